```python
import math
import jax
import jax.numpy as jnp
from jax import lax
import numpy as np

D_MODEL = 1024
BATCH = 4
SEQ = 8192
DEPTH = 1
DEC_BATCH = 128
DEC_SEQ = 1
PAST_LEN = 8192
PAGE_SIZE = 128

N_META = 16
H_A = 4
D_HEAD = 64
ROPE_THETA = 10000.0
Q_BLOCK = 128
H_B = 4
DQK = 64
DV = 128
CONV_W = 4
CHUNK = 128
D_FF = 2816
EPS = 1e-6

W_AQ = 2 * H_A * D_HEAD
W_AV = H_A * 2 * D_HEAD
W_BQK = H_B * DQK
W_BV = H_B * DV
COL_SIZES = (W_AQ, W_AQ, W_AV, 2 * W_BQK, W_BV, W_BV, 2 * H_B, 2 * D_MODEL)
D_IN = 2 * W_AQ + W_AV + 2 * W_BQK + 2 * W_BV + 2 * H_B + 2 * D_MODEL
ATTN_SCALE = D_HEAD ** -0.5

kernel_name = 'hybrid_diffattn_mlstm_macaron_step'


def _rmsnorm(x, g):
    xf = x.astype(jnp.float32)
    y = xf * lax.rsqrt(jnp.mean(xf * xf, axis=-1, keepdims=True) + EPS)
    return y.astype(x.dtype) * g


def _ffn_half(x, norm_g, w_in, w_out):
    gu = _rmsnorm(x, norm_g) @ w_in
    g, u = jnp.split(gu, 2, axis=-1)
    return x + 0.5 * ((jax.nn.silu(g) * u) @ w_out)


def _rope(x, pos):
    half = D_HEAD // 2
    inv = 1.0 / (ROPE_THETA ** (jnp.arange(half, dtype=jnp.float32) / half))
    ang = pos.astype(jnp.float32)[:, None] * inv[None, :]
    cos = jnp.cos(ang)[None, :, None, :]
    sin = jnp.sin(ang)[None, :, None, :]
    xf = x.astype(jnp.float32)
    x1, x2 = xf[..., :half], xf[..., half:]
    return jnp.concatenate([x1 * cos - x2 * sin, x2 * cos + x1 * sin], axis=-1).astype(x.dtype)


def _project(h, pos, lp):
    bt, t, _ = h.shape
    z = h @ lp['w_in']
    offs = []
    acc = 0
    for c in COL_SIZES[:-1]:
        acc += c
        offs.append(acc)
    aq, ak, av, u, bv, bo, bif, g = jnp.split(z, offs, axis=-1)
    qa = _rope(_rmsnorm(aq.reshape(bt, t, 2 * H_A, D_HEAD), lp['q_norm']), pos)
    ka = _rope(_rmsnorm(ak.reshape(bt, t, 2 * H_A, D_HEAD), lp['k_norm']), pos)
    va = av.reshape(bt, t, H_A, 2 * D_HEAD)
    bif = (bif + lp['b_if']).astype(jnp.float32)
    ig = bif[..., :H_B]
    logf = jax.nn.log_sigmoid(bif[..., H_B:])
    g = jax.nn.sigmoid(g + lp['b_gate'])
    ga, gb = jnp.split(g, 2, axis=-1)
    return qa, ka, va, u, bv.reshape(bt, t, H_B, DV), bo, ig, logf, ga, gb


def _lambda(lp, lam_init):
    f = jnp.float32
    return (jnp.exp(jnp.sum(lp['lam_q1'].astype(f) * lp['lam_k1'].astype(f)))
            - jnp.exp(jnp.sum(lp['lam_q2'].astype(f) * lp['lam_k2'].astype(f))) + lam_init)


def _diff_attend(q, segments, lam):
    scores = []
    for k, _, mask in segments:
        s = jnp.einsum('bqhd,bkhd->bhqk', q, k).astype(jnp.float32) * ATTN_SCALE
        if mask is not None:
            s = jnp.where(mask, s, -jnp.inf)
        scores.append(s)
    p = jax.nn.softmax(jnp.concatenate(scores, axis=-1), axis=-1)
    b, _, nq, kt = p.shape
    p = p.reshape(b, H_A, 2, nq, kt)
    a = p[:, :, 0] - lam * p[:, :, 1]
    out = None
    off = 0
    for k, v, _ in segments:
        kl = k.shape[1]
        term = jnp.einsum('bhqk,bkhe->bqhe', a[..., off:off + kl].astype(v.dtype), v)
        out = term if out is None else out + term
        off += kl
    return out


def _short_conv(u_ext, w, b, t):
    out = b + u_ext[:, 0:t] * w[0]
    for j in range(1, CONV_W):
        out = out + u_ext[:, j:j + t] * w[j]
    return jax.nn.silu(out)


def _mlstm_qk(uc):
    bt, t, _ = uc.shape
    q = uc[..., :W_BQK].reshape(bt, t, H_B, DQK)
    k = uc[..., W_BQK:].reshape(bt, t, H_B, DQK) * (DQK ** -0.5)
    return q, k


def _mlstm_chunk(q, k, v, ig, logf, c0, n0, m0):
    f32 = jnp.float32
    qf, kf, vf = q.astype(f32), k.astype(f32), v.astype(f32)
    el = q.shape[1]
    fh = jnp.moveaxis(jnp.cumsum(logf, axis=1), 1, 2)
    igh = jnp.moveaxis(ig, 1, 2)
    causal = jnp.tril(jnp.ones((el, el), dtype=bool))
    dmat = jnp.where(causal, fh[..., :, None] - fh[..., None, :] + igh[..., None, :], -jnp.inf)
    b_in = fh + m0[..., None]
    m = jnp.maximum(b_in, jnp.max(dmat, axis=-1))
    wts = jnp.exp(dmat - m[..., None])
    inter = jnp.exp(b_in - m)
    s = jnp.einsum('blhd,bshd->bhls', qf, kf) * wts
    num = jnp.einsum('bhls,bshv->bhlv', s, vf) + inter[..., None] * jnp.einsum('bhvd,blhd->bhlv', c0, qf)
    den = jnp.sum(s, axis=-1) + inter * jnp.einsum('bhd,blhd->bhl', n0, qf)
    h = num / jnp.maximum(jnp.abs(den), jnp.exp(-m))[..., None]
    m_new = m[..., -1]
    wl = jnp.exp(fh[..., -1:] - fh + igh - m_new[..., None])
    decay = jnp.exp(fh[..., -1] + m0 - m_new)
    c_new = decay[..., None, None] * c0 + jnp.einsum('bhs,bshv,bshd->bhvd', wl, vf, kf)
    n_new = decay[..., None] * n0 + jnp.einsum('bhs,bshd->bhd', wl, kf)
    return jnp.moveaxis(h, 1, 2), (c_new, n_new, m_new)


def _mixer_out(oa, hb, bo, ga, gb, lp, lam_init, dtype):
    bt, t = oa.shape[:2]
    ya = (_rmsnorm(oa, lp['sub_norm']) * (1.0 - lam_init)).reshape(bt, t, W_AV)
    yb = _rmsnorm(hb.astype(dtype), lp['mh_norm'].reshape(H_B, DV)).reshape(bt, t, W_BV) * jax.nn.sigmoid(bo)
    merged = ga * (ya @ lp['p_a']) + gb * (yb @ lp['p_b'])
    return merged @ lp['w_o']


def _prompt_layer(x, pos, lp, lam_init):
    bt, t, _ = x.shape
    s = t - N_META
    x = _ffn_half(x, lp['ffn1_norm'], lp['ffn1_w_in'], lp['ffn1_w_out'])
    h = _rmsnorm(x, lp['mix_norm'])
    qa, ka, va, u, vb, bo, ig, logf, ga, gb = _project(h, pos, lp)
    lam = _lambda(lp, lam_init)
    mpos = pos[:N_META]
    oa_meta = _diff_attend(qa[:, :N_META], [(ka[:, :N_META], va[:, :N_META], mpos[:, None] >= mpos[None, :])], lam)
    nqb = s // Q_BLOCK
    qblk = jnp.swapaxes(qa[:, N_META:].reshape(bt, nqb, Q_BLOCK, 2 * H_A, D_HEAD), 0, 1)
    pblk = pos[N_META:].reshape(nqb, Q_BLOCK)
    oa_real = lax.map(lambda qp: _diff_attend(qp[0], [(ka, va, qp[1][:, None] >= pos[None, :])], lam), (qblk, pblk))
    oa_real = jnp.swapaxes(oa_real, 0, 1).reshape(bt, s, H_A, 2 * D_HEAD)
    oa = jnp.concatenate([oa_meta, oa_real], axis=1)
    u_ext = jnp.pad(u, ((0, 0), (CONV_W - 1, 0), (0, 0)))
    qb, kb = _mlstm_qk(_short_conv(u_ext, lp['conv_w'], lp['conv_b'], t))
    c0 = jnp.zeros((bt, H_B, DV, DQK), jnp.float32)
    n0 = jnp.zeros((bt, H_B, DQK), jnp.float32)
    m0 = jnp.full((bt, H_B), -jnp.inf, jnp.float32)
    hb_meta, st = _mlstm_chunk(qb[:, :N_META], kb[:, :N_META], vb[:, :N_META], ig[:, :N_META], logf[:, :N_META], c0, n0, m0)
    nc = s // CHUNK

    def to_chunks(a):
        return jnp.swapaxes(a[:, N_META:].reshape((bt, nc, CHUNK) + a.shape[2:]), 0, 1)

    def step(carry, xs):
        hc, carry = _mlstm_chunk(*xs, *carry)
        return carry, hc

    st, hb_real = lax.scan(step, st, (to_chunks(qb), to_chunks(kb), to_chunks(vb), to_chunks(ig), to_chunks(logf)))
    hb_real = jnp.swapaxes(hb_real, 0, 1).reshape(bt, s, H_B, DV)
    hb = jnp.concatenate([hb_meta, hb_real], axis=1)
    x = x + _mixer_out(oa, hb, bo, ga, gb, lp, lam_init, x.dtype)
    x = _ffn_half(x, lp['ffn2_norm'], lp['ffn2_w_in'], lp['ffn2_w_out'])
    return x, (ka, va, st[0], st[1], st[2], u[:, t - (CONV_W - 1):])


def _sample_layer(x, k_pool, v_pool, layer, page_table, c0, n0, m0, conv0, lp, lam_init):
    bt, nq, _ = x.shape
    past = page_table.shape[1] * PAGE_SIZE
    pos = past + jnp.arange(nq)
    x = _ffn_half(x, lp['ffn1_norm'], lp['ffn1_w_in'], lp['ffn1_w_out'])
    h = _rmsnorm(x, lp['mix_norm'])
    qa, ka, va, u, vb, bo, ig, logf, ga, gb = _project(h, pos, lp)
    lam = _lambda(lp, lam_init)
    k_past = k_pool[layer, page_table].reshape(bt, past, 2 * H_A, D_HEAD)
    v_past = v_pool[layer, page_table].reshape(bt, past, H_A, 2 * D_HEAD)
    oa = _diff_attend(qa, [(k_past, v_past, None), (ka, va, pos[:, None] >= pos[None, :])], lam)
    u_ext = jnp.concatenate([conv0.astype(u.dtype), u], axis=1)
    qb, kb = _mlstm_qk(_short_conv(u_ext, lp['conv_w'], lp['conv_b'], nq))
    hb, st = _mlstm_chunk(qb, kb, vb, ig, logf, c0.astype(jnp.float32), n0.astype(jnp.float32), m0.astype(jnp.float32))
    x = x + _mixer_out(oa, hb, bo, ga, gb, lp, lam_init, x.dtype)
    x = _ffn_half(x, lp['ffn2_norm'], lp['ffn2_w_in'], lp['ffn2_w_out'])
    return x, (ka, va, st[0], st[1], st[2], u_ext[:, nq:])


def setup_inputs(seed: int = 0) -> dict:
    key = jax.random.key(seed)
    ks = iter(jax.random.split(key, 48))
    f32 = jnp.float32

    def nrm(shape, scale=1.0):
        return jax.random.normal(next(ks), shape, f32) * scale

    n_pages = PAST_LEN // PAGE_SIZE
    n_pool = (DEC_BATCH * n_pages * 5) // 4
    el = DEPTH
    page_table = jax.random.permutation(next(ks), n_pool)[:DEC_BATCH * n_pages].reshape(DEC_BATCH, n_pages).astype(jnp.int32)
    b_i = nrm((el, H_B), 0.01)
    b_f = 3.0 + jnp.arange(H_B, dtype=f32)[None, :] + nrm((el, H_B), 0.1)
    return {
        'x_prompt': nrm((BATCH, SEQ, D_MODEL)),
        'x_sample': nrm((DEC_BATCH, DEC_SEQ, D_MODEL)),
        'cache_k': nrm((el, n_pool, PAGE_SIZE, 2 * H_A, D_HEAD)),
        'cache_v': nrm((el, n_pool, PAGE_SIZE, H_A, 2 * D_HEAD)),
        'page_table': page_table,
        'state_C': nrm((el, DEC_BATCH, H_B, DV, DQK)),
        'state_n': nrm((el, DEC_BATCH, H_B, DQK)),
        'state_m': nrm((el, DEC_BATCH, H_B)),
        'state_conv': nrm((el, DEC_BATCH, CONV_W - 1, 2 * W_BQK)),
        'meta_tokens': nrm((N_META, D_MODEL)),
        'ffn1_norm': 1.0 + nrm((el, D_MODEL), 0.02),
        'ffn1_w_in': nrm((el, D_MODEL, 2 * D_FF), D_MODEL ** -0.5),
        'ffn1_w_out': nrm((el, D_FF, D_MODEL), D_FF ** -0.5),
        'mix_norm': 1.0 + nrm((el, D_MODEL), 0.02),
        'w_in': nrm((el, D_MODEL, D_IN), D_MODEL ** -0.5),
        'b_if': jnp.concatenate([b_i, b_f], axis=-1),
        'b_gate': nrm((el, 2 * D_MODEL), 0.01),
        'q_norm': 1.0 + nrm((el, D_HEAD), 0.02),
        'k_norm': 1.0 + nrm((el, D_HEAD), 0.02),
        'lam_q1': nrm((el, D_HEAD), 0.1),
        'lam_k1': nrm((el, D_HEAD), 0.1),
        'lam_q2': nrm((el, D_HEAD), 0.1),
        'lam_k2': nrm((el, D_HEAD), 0.1),
        'sub_norm': 1.0 + nrm((el, 2 * D_HEAD), 0.02),
        'conv_w': nrm((el, CONV_W, 2 * W_BQK), CONV_W ** -0.5),
        'conv_b': nrm((el, 2 * W_BQK), 0.01),
        'mh_norm': 1.0 + nrm((el, W_BV), 0.02),
        'p_a': nrm((el, W_AV, D_MODEL), W_AV ** -0.5),
        'p_b': nrm((el, W_BV, D_MODEL), W_BV ** -0.5),
        'w_o': nrm((el, D_MODEL, D_MODEL), D_MODEL ** -0.5),
        'ffn2_norm': 1.0 + nrm((el, D_MODEL), 0.02),
        'ffn2_w_in': nrm((el, D_MODEL, 2 * D_FF), D_MODEL ** -0.5),
        'ffn2_w_out': nrm((el, D_FF, D_MODEL), D_FF ** -0.5),
    }


def reference(x_prompt, x_sample, cache_k, cache_v, page_table, state_C, state_n, state_m, state_conv,
              meta_tokens, ffn1_norm, ffn1_w_in, ffn1_w_out, mix_norm, w_in, b_if, b_gate, q_norm, k_norm,
              lam_q1, lam_k1, lam_q2, lam_k2, sub_norm, conv_w, conv_b, mh_norm, p_a, p_b, w_o,
              ffn2_norm, ffn2_w_in, ffn2_w_out):
    bp, s, _ = x_prompt.shape
    pos_p = jnp.arange(N_META + s)
    meta = jnp.broadcast_to(meta_tokens.astype(x_prompt.dtype)[None], (bp, N_META, D_MODEL))
    xp = jnp.concatenate([meta, x_prompt], axis=1)
    xs = x_sample
    outs_p = []
    outs_s = []
    for l in range(DEPTH):
        lam_init = 0.8 - 0.6 * math.exp(-0.3 * l)
        lp = dict(ffn1_norm=ffn1_norm[l], ffn1_w_in=ffn1_w_in[l], ffn1_w_out=ffn1_w_out[l], mix_norm=mix_norm[l],
                  w_in=w_in[l], b_if=b_if[l], b_gate=b_gate[l], q_norm=q_norm[l], k_norm=k_norm[l],
                  lam_q1=lam_q1[l], lam_k1=lam_k1[l], lam_q2=lam_q2[l], lam_k2=lam_k2[l], sub_norm=sub_norm[l],
                  conv_w=conv_w[l], conv_b=conv_b[l], mh_norm=mh_norm[l], p_a=p_a[l], p_b=p_b[l], w_o=w_o[l],
                  ffn2_norm=ffn2_norm[l], ffn2_w_in=ffn2_w_in[l], ffn2_w_out=ffn2_w_out[l])
        xp, st_p = _prompt_layer(xp, pos_p, lp, lam_init)
        xs, st_s = _sample_layer(xs, cache_k, cache_v, l, page_table, state_C[l], state_n[l], state_m[l],
                                 state_conv[l], lp, lam_init)
        outs_p.append(st_p)
        outs_s.append(st_s)
    k_p, v_p, c_p, n_p, m_p, conv_p = [jnp.stack(a) for a in zip(*outs_p)]
    k_s, v_s, c_s, n_s, m_s, conv_s = [jnp.stack(a) for a in zip(*outs_s)]
    return (xp[:, N_META:], xs, k_p, v_p, k_s, v_s, c_p, n_p, m_p, conv_p, c_s, n_s, m_s, conv_s)
```

```python
import functools

import jax
import jax.numpy as jnp
from jax import lax
from jax.experimental import pallas as pl
from jax.experimental.pallas import tpu as pltpu

F32 = jnp.float32
BF16 = jnp.bfloat16

N_META = 16
H_A = 4
D_HEAD = 64
H_B = 4
DQK = 64
DV = 128
CONV_W = 4
CHUNK = 128
ROPE_THETA = 10000.0
EPS = 1e-6
ATTN_SCALE = D_HEAD ** -0.5
LAM_INIT = 0.2
LANES = 128
W_A = 2 * H_A * D_HEAD
W_U = 2 * H_B * DQK
W_BV = H_B * DV
VMEM_LIMIT = 56 * 1024 * 1024
NEG_INF = float("-inf")


def _dot(a, b):
    return jnp.dot(a, b, preferred_element_type=F32)


def _nt(a, b):
    return lax.dot_general(a, b, (((1,), (1,)), ((), ())), preferred_element_type=F32)


def _rms(x):
    return x * lax.rsqrt(jnp.mean(x * x, axis=-1, keepdims=True) + EPS)


def _sigmoid(x):
    return 1.0 / (1.0 + jnp.exp(-x))


def _log_sigmoid(x):
    return jnp.minimum(x, 0.0) - jnp.log1p(jnp.exp(-jnp.abs(x)))


def _split3(x):
    x1 = x.astype(BF16)
    r = x - x1.astype(F32)
    x2 = r.astype(BF16)
    x3 = (r - x2.astype(F32)).astype(BF16)
    return x1, x2, x3


def _params(n_axes):
    return pltpu.CompilerParams(dimension_semantics=("arbitrary",) * n_axes,
                                vmem_limit_bytes=VMEM_LIMIT)


def _const_spec(shape):
    nd = len(shape)
    return pl.BlockSpec(shape, lambda *_: (0,) * nd, pipeline_mode=pl.Buffered(1))


def _ffn_kernel(x_ref, g_ref, win_ref, wout_ref, o_ref, *, d_ff, ck):
    x = x_ref[...]
    h = (_rms(x) * g_ref[...]).astype(BF16)
    acc = jnp.zeros(x.shape, F32)
    for c in range(d_ff // ck):
        g = _dot(h, win_ref[:, c * ck:(c + 1) * ck])
        u = _dot(h, win_ref[:, d_ff + c * ck:d_ff + (c + 1) * ck])
        a = (g * _sigmoid(g) * u).astype(BF16)
        acc = acc + _dot(a, wout_ref[c * ck:(c + 1) * ck, :])
    o_ref[...] = x + 0.5 * acc


def _ffn(x, norm_g, w_in, w_out, tm):
    m, d = x.shape
    d_ff = w_out.shape[0]
    ck = 256
    assert m % tm == 0 and d_ff % ck == 0
    return pl.pallas_call(
        functools.partial(_ffn_kernel, d_ff=d_ff, ck=ck),
        grid=(m // tm,),
        in_specs=[pl.BlockSpec((tm, d), lambda i: (i, 0)),
                  _const_spec((1, d)),
                  _const_spec((d, 2 * d_ff)),
                  _const_spec((d_ff, d))],
        out_specs=pl.BlockSpec((tm, d), lambda i: (i, 0)),
        out_shape=jax.ShapeDtypeStruct((m, d), F32),
        compiler_params=_params(1),
        name="ffn",
    )(x, norm_g, w_in, w_out)


def _proj_kernel(x_ref, g_ref, wqk_ref, wv_ref, wu_ref, wbv_ref, wif_ref, bif_ref,
                 qn_ref, kn_ref, cos_ref, sin_ref, gmat_ref,
                 q_ref, k_ref, kb_ref, v_ref, vb_ref, u_ref, bv_ref, gc_ref, gr_ref):
    x = x_ref[...]
    tm = x.shape[0]
    h = (_rms(x) * g_ref[...]).astype(BF16)
    cos = cos_ref[...]
    sin = sin_ref[...]
    gmat = gmat_ref[...]
    lane = lax.broadcasted_iota(jnp.int32, (tm, LANES), 1)
    first_half = (lane % D_HEAD) < (D_HEAD // 2)

    def norm_rope(z, nw):
        sq = z * z
        hi = sq.astype(BF16)
        lo = (sq - hi.astype(F32)).astype(BF16)
        ms = _dot(hi, gmat) + _dot(lo, gmat)
        y = z * lax.rsqrt(ms + EPS) * nw
        rot = jnp.where(first_half, pltpu.roll(y, LANES - D_HEAD // 2, 1), pltpu.roll(y, D_HEAD // 2, 1))
        return y * cos + rot * sin

    qn = qn_ref[...]
    kn = kn_ref[...]
    for s in range(W_A // LANES):
        sl = slice(s * LANES, (s + 1) * LANES)
        zq = _dot(h, wqk_ref[:, s * LANES:(s + 1) * LANES])
        q_ref[:, sl] = (norm_rope(zq, qn) * ATTN_SCALE).astype(BF16)
        zk = _dot(h, wqk_ref[:, W_A + s * LANES:W_A + (s + 1) * LANES])
        kk = norm_rope(zk, kn)
        k_ref[:, sl] = kk
        kb_ref[:, sl] = kk.astype(BF16)
    v = _dot(h, wv_ref[...])
    v_ref[...] = v
    vb_ref[...] = v.astype(BF16)
    u_ref[...] = _dot(h, wu_ref[...])
    bv_ref[...] = _dot(h, wbv_ref[...])
    z = _dot(h, wif_ref[...]) + bif_ref[...]
    gc = jnp.where(lane < H_B, z, _log_sigmoid(z))
    gc_ref[...] = gc
    gr_ref[...] = gc.T[:8, :]


def _proj(x, mix_norm, wp, cos_t, sin_t, tm, pos_blocks):
    m, d = x.shape
    assert m % tm == 0
    row = lambda w: pl.BlockSpec((tm, w), lambda i: (i, 0))
    tab = pl.BlockSpec((tm, LANES), lambda i: (i % pos_blocks, 0))
    outs = [
        jax.ShapeDtypeStruct((m, W_A), BF16),
        jax.ShapeDtypeStruct((m, W_A), F32),
        jax.ShapeDtypeStruct((m, W_A), BF16),
        jax.ShapeDtypeStruct((m, W_A), F32),
        jax.ShapeDtypeStruct((m, W_A), BF16),
        jax.ShapeDtypeStruct((m, W_U), F32),
        jax.ShapeDtypeStruct((m, W_BV), F32),
        jax.ShapeDtypeStruct((m, LANES), F32),
        jax.ShapeDtypeStruct((8, m), F32),
    ]
    return pl.pallas_call(
        _proj_kernel,
        grid=(m // tm,),
        in_specs=[row(d), _const_spec((1, d)),
                  _const_spec((d, 2 * W_A)), _const_spec((d, W_A)), _const_spec((d, W_U)),
                  _const_spec((d, W_BV)), _const_spec((d, LANES)), _const_spec((1, LANES)),
                  _const_spec((1, LANES)), _const_spec((1, LANES)), tab, tab,
                  _const_spec((LANES, LANES))],
        out_specs=[row(W_A), row(W_A), row(W_A), row(W_A), row(W_A), row(W_U), row(W_BV),
                   row(LANES), pl.BlockSpec((8, tm), lambda i: (0, i))],
        out_shape=outs,
        compiler_params=_params(1),
        name="proj",
    )(x, mix_norm, wp["wqk"], wp["wv"], wp["wu"], wp["wbv"], wp["wif"], wp["bif"],
      wp["qn"], wp["kn"], cos_t, sin_t, wp["gmat"])


def _lambda(lam_ref):
    a = jnp.sum(lam_ref[0:1, :] * lam_ref[1:2, :], axis=-1, keepdims=True)
    b = jnp.sum(lam_ref[2:3, :] * lam_ref[3:4, :], axis=-1, keepdims=True)
    return jnp.exp(a) - jnp.exp(b) + LAM_INIT


def _attn_kernel(q_ref, k_ref, v_ref, km_ref, vm_ref, lam_ref, sn_ref, o_ref, *, tq):
    i = pl.program_id(2)
    q = q_ref[0]
    lane = lax.broadcasted_iota(jnp.int32, (tq, LANES), 1)
    zero = jnp.zeros_like(q)
    qq = jnp.concatenate([jnp.where(lane < D_HEAD, q, zero), jnp.where(lane >= D_HEAD, q, zero)], axis=0)

    def update(carry, s, v):
        m, l, acc = carry
        m_new = jnp.maximum(m, jnp.max(s, axis=1, keepdims=True))
        alpha = jnp.exp(m - m_new)
        p = jnp.exp(s - m_new)
        l = alpha * l + jnp.sum(p, axis=1, keepdims=True)
        acc = alpha * acc + _dot(p.astype(BF16), v)
        return m_new, l, acc

    s = _nt(qq, km_ref[...])
    col = lax.broadcasted_iota(jnp.int32, s.shape, 1)
    s = jnp.where(col < N_META, s, NEG_INF)
    m = jnp.max(s, axis=1, keepdims=True)
    p = jnp.exp(s - m)
    carry = (m, jnp.sum(p, axis=1, keepdims=True), _dot(p.astype(BF16), vm_ref[...]))

    def body(j, carry):
        off = pl.multiple_of(j * tq, tq)
        k = k_ref[0, pl.ds(off, tq), :]
        v = v_ref[0, pl.ds(off, tq), :]
        return update(carry, _nt(qq, k), v)

    carry = lax.fori_loop(0, i, body, carry)

    off = pl.multiple_of(i * tq, tq)
    k = k_ref[0, pl.ds(off, tq), :]
    v = v_ref[0, pl.ds(off, tq), :]
    s = _nt(qq, k)
    row = lax.broadcasted_iota(jnp.int32, s.shape, 0)
    colk = lax.broadcasted_iota(jnp.int32, s.shape, 1)
    qidx = jnp.where(row >= tq, row - tq, row)
    s = jnp.where(colk <= qidx, s, NEG_INF)
    m, l, acc = update(carry, s, v)

    o = acc / l
    oa = o[:tq] - _lambda(lam_ref) * o[tq:]
    o_ref[0] = (_rms(oa) * sn_ref[...] * (1.0 - LAM_INIT)).astype(BF16)


def _prompt_attn(q, kb, vb, kmeta, vmeta, lam, sub_norm, tq):
    b, s, _ = q.shape
    assert s % tq == 0
    return pl.pallas_call(
        functools.partial(_attn_kernel, tq=tq),
        grid=(b, H_A, s // tq),
        in_specs=[pl.BlockSpec((1, tq, LANES), lambda b, h, i: (b, i, h)),
                  pl.BlockSpec((1, s, LANES), lambda b, h, i: (b, 0, h)),
                  pl.BlockSpec((1, s, LANES), lambda b, h, i: (b, 0, h)),
                  pl.BlockSpec((LANES, LANES), lambda b, h, i: (0, h)),
                  pl.BlockSpec((LANES, LANES), lambda b, h, i: (0, h)),
                  pl.BlockSpec((4, D_HEAD), lambda b, h, i: (0, 0)),
                  pl.BlockSpec((1, LANES), lambda b, h, i: (0, 0))],
        out_specs=pl.BlockSpec((1, tq, LANES), lambda b, h, i: (b, i, h)),
        out_shape=jax.ShapeDtypeStruct((b, s, W_A), BF16),
        compiler_params=_params(3),
        name="prompt_attn",
    )(q, kb, vb, kmeta, vmeta, lam, sub_norm)


def _mlstm_kernel(u_ref, bv_ref, gc_ref, gr_ref, cw_ref, cb_ref, c0_ref, n0_ref, m0_ref, t0_ref,
                  hb_ref, c_ref, n_ref, m_ref, t_ref, ue_ref, *, el):
    c = pl.program_id(1)

    @pl.when(c == 0)
    def _():
        c_ref[...] = c0_ref[...]
        n_ref[...] = n0_ref[...]
        m_ref[...] = m0_ref[...]
        t_ref[...] = t0_ref[...]

    ue_ref[0:8, :] = t_ref[0]
    ue_ref[8:8 + el, :] = u_ref[0]
    conv = cb_ref[...] + ue_ref[5:5 + el, :] * cw_ref[0:1, :]
    for j in range(1, CONV_W):
        conv = conv + ue_ref[5 + j:5 + j + el, :] * cw_ref[j:j + 1, :]
    t_ref[0] = ue_ref[el:el + 8, :]
    qk = conv * _sigmoid(conv)

    gc = gc_ref[0]
    gr = gr_ref[...]
    ri = lax.broadcasted_iota(jnp.int32, (el, el), 0)
    ci = lax.broadcasted_iota(jnp.int32, (el, el), 1)
    causal = ci <= ri
    tri = jnp.where(causal, 1.0, 0.0).astype(BF16)
    tri_t = jnp.where(ri <= ci, 1.0, 0.0).astype(BF16)
    g1, g2, g3 = _split3(gc)
    fh_cols = _dot(tri, g1) + _dot(tri, g2) + _dot(tri, g3)
    r1, r2, r3 = _split3(gr)
    fh_rows = _dot(r1, tri_t) + _dot(r2, tri_t) + _dot(r3, tri_t)

    lane = lax.broadcasted_iota(jnp.int32, (el, LANES), 1)
    for h in range(H_B):
        pair = h // 2
        mine = (lane >= DQK) if (h % 2) else (lane < DQK)
        qp = qk[:, pair * LANES:(pair + 1) * LANES]
        kp = qk[:, H_B * DQK + pair * LANES:H_B * DQK + (pair + 1) * LANES] * (DQK ** -0.5)
        km = jnp.where(mine, kp, 0.0)
        qm = jnp.where(mine, qp, 0.0)
        vh = bv_ref[0, :, h * DV:(h + 1) * DV]
        fhc = fh_cols[:, H_B + h:H_B + h + 1]
        fhr = fh_rows[H_B + h:H_B + h + 1, :]
        igc = gc[:, h:h + 1]
        igr = gr[h:h + 1, :]
        m0 = m_ref[0, h:h + 1, 0:1]
        c0 = c_ref[0, h]
        n0 = n_ref[0, h:h + 1, :]

        dmat = jnp.where(causal, fhc - fhr + igr, NEG_INF)
        b_in = fhc + m0
        m = jnp.maximum(b_in, jnp.max(dmat, axis=1, keepdims=True))
        wts = jnp.exp(dmat - m)
        inter = jnp.exp(b_in - m)
        qmb = qm.astype(BF16)
        s = _nt(qmb, kp.astype(BF16)) * wts
        num = _dot(s.astype(BF16), vh.astype(BF16)) + inter * _nt(qmb, c0.astype(BF16))
        den = jnp.sum(s, axis=1, keepdims=True) + inter * jnp.sum(qm * n0, axis=1, keepdims=True)
        hout = num / jnp.maximum(jnp.abs(den), jnp.exp(-m))
        hb_ref[0, :, h * DV:(h + 1) * DV] = _rms(hout)

        m_new = m[el - 1:el, :]
        fl = fhc[el - 1:el, :]
        wl = jnp.exp(fl - fhc + igc - m_new)
        decay = jnp.exp(fl + m0 - m_new)
        vw = (vh * wl).T.astype(BF16)
        c_ref[0, h] = decay * c0 + _dot(vw, km.astype(BF16))
        n_ref[0, h:h + 1, :] = decay * n0 + jnp.sum(wl * km, axis=0, keepdims=True)
        m_ref[0, h:h + 1, :] = jnp.broadcast_to(m_new, (1, LANES))


def _mlstm(u, bv, gc, gr, conv_w, conv_b, c0, n0, m0, t0, el):
    b, s, _ = u.shape
    nc = s // el
    assert s % el == 0 and gr.shape == (8, b * s)
    st = lambda shape: pl.BlockSpec((1,) + shape, lambda b, c: (b,) + (0,) * len(shape))
    seq = lambda w: pl.BlockSpec((1, el, w), lambda b, c: (b, c, 0))
    outs = [jax.ShapeDtypeStruct((b, s, W_BV), F32),
            jax.ShapeDtypeStruct((b, H_B, DV, LANES), F32),
            jax.ShapeDtypeStruct((b, H_B, LANES), F32),
            jax.ShapeDtypeStruct((b, H_B, LANES), F32),
            jax.ShapeDtypeStruct((b, 8, W_U), F32)]
    return pl.pallas_call(
        functools.partial(_mlstm_kernel, el=el),
        grid=(b, nc),
        in_specs=[seq(W_U), seq(W_BV), seq(LANES),
                  pl.BlockSpec((8, el), lambda b, c: (0, b * nc + c)),
                  pl.BlockSpec((CONV_W, W_U), lambda b, c: (0, 0)),
                  pl.BlockSpec((1, W_U), lambda b, c: (0, 0)),
                  st((H_B, DV, LANES)), st((H_B, LANES)), st((H_B, LANES)), st((8, W_U))],
        out_specs=[seq(W_BV), st((H_B, DV, LANES)), st((H_B, LANES)), st((H_B, LANES)), st((8, W_U))],
        out_shape=outs,
        scratch_shapes=[pltpu.VMEM((el + 8, W_U), F32)],
        compiler_params=_params(2),
        name="mlstm_scan",
    )(u, bv, gc, gr, conv_w, conv_b, c0, n0, m0, t0)


def _mixer_kernel(x_ref, ya_ref, hb_ref, g_ref, wg_ref, bg_ref, wbo_ref, mh_ref, pa_ref, pb_ref, wo_ref, o_ref):
    x = x_ref[...]
    d = x.shape[1]
    h = (_rms(x) * g_ref[...]).astype(BF16)
    bo = _dot(h, wbo_ref[...])
    yb = (hb_ref[...] * mh_ref[...] * _sigmoid(bo)).astype(BF16)
    pa = _dot(ya_ref[...], pa_ref[...])
    pb = _dot(yb, pb_ref[...])
    ga = _sigmoid(_dot(h, wg_ref[:, :d]) + bg_ref[:, :d])
    gb = _sigmoid(_dot(h, wg_ref[:, d:]) + bg_ref[:, d:])
    merged = (ga * pa + gb * pb).astype(BF16)
    o_ref[...] = x + _dot(merged, wo_ref[...])


def _mixer(x, ya, hbn, mix_norm, wp, tm):
    m, d = x.shape
    assert m % tm == 0
    row = lambda w: pl.BlockSpec((tm, w), lambda i: (i, 0))
    return pl.pallas_call(
        _mixer_kernel,
        grid=(m // tm,),
        in_specs=[row(d), row(W_A), row(W_BV), _const_spec((1, d)),
                  _const_spec((d, 2 * d)), _const_spec((1, 2 * d)), _const_spec((d, W_BV)),
                  _const_spec((1, W_BV)), _const_spec((W_A, d)), _const_spec((W_BV, d)),
                  _const_spec((d, d))],
        out_specs=row(d),
        out_shape=jax.ShapeDtypeStruct((m, d), F32),
        compiler_params=_params(1),
        name="mixer_out",
    )(x, ya, hbn, mix_norm, wp["wg"], wp["bg"], wp["wbo"], wp["mh"], wp["pa"], wp["pb"], wp["wo"])


def _decode_kernel(pt_ref, q_ref, kn_ref, vn_ref, lam_ref, sn_ref, *refs, pps):
    k_refs = refs[:pps]
    v_refs = refs[pps:2 * pps]
    o_ref, m_s, l_s, acc_s = refs[2 * pps:]
    g = pl.program_id(1)
    ng = pl.num_programs(1)
    nsub = 2 * H_A

    @pl.when(g == 0)
    def _():
        m_s[...] = jnp.full(m_s.shape, NEG_INF, F32)
        l_s[...] = jnp.zeros(l_s.shape, F32)
        acc_s[...] = jnp.zeros(acc_s.shape, F32)

    q = q_ref[0]
    row = lax.broadcasted_iota(jnp.int32, (nsub, W_A), 0)
    lane = lax.broadcasted_iota(jnp.int32, (nsub, W_A), 1)
    qbd = jnp.where(lane // D_HEAD == row, jnp.broadcast_to(q.astype(F32), (nsub, W_A)), 0.0).astype(BF16)

    def update(s, pv_fn):
        m = m_s[...]
        m_new = jnp.maximum(m, jnp.max(s, axis=1, keepdims=True))
        alpha = jnp.exp(m - m_new)
        p = jnp.exp(s - m_new)
        l_s[...] = alpha * l_s[...] + jnp.sum(p, axis=1, keepdims=True)
        acc_s[...] = alpha * acc_s[...] + pv_fn(p.astype(BF16))
        m_s[...] = m_new

    s = jnp.concatenate([_nt(qbd, k_refs[i][...].astype(BF16)) for i in range(pps)], axis=1)
    page = k_refs[0].shape[0]

    def pv_pages(p):
        out = _dot(p[:, 0:page], v_refs[0][...].astype(BF16))
        for i in range(1, pps):
            out = out + _dot(p[:, i * page:(i + 1) * page], v_refs[i][...].astype(BF16))
        return out

    update(s, pv_pages)

    @pl.when(g == ng - 1)
    def _():
        k_self = jnp.broadcast_to(kn_ref[0], (8, W_A)).astype(BF16)
        v_self = jnp.broadcast_to(vn_ref[0], (8, W_A)).astype(BF16)
        s_self = _nt(qbd, k_self)
        c8 = lax.broadcasted_iota(jnp.int32, s_self.shape, 1)
        update(jnp.where(c8 == 0, s_self, NEG_INF), lambda p: _dot(p, v_self))
        o = acc_s[...] / l_s[...]
        lam = _lambda(lam_ref)
        outs = []
        for h in range(H_A):
            sl = slice(h * 2 * D_HEAD, (h + 1) * 2 * D_HEAD)
            oa = o[2 * h:2 * h + 1, sl] - lam * o[2 * h + 1:2 * h + 2, sl]
            outs.append(_rms(oa) * sn_ref[...] * (1.0 - LAM_INIT))
        o_ref[0] = jnp.concatenate(outs, axis=1).astype(BF16)


def _decode_attn(page_table, q, k_new, v_new, cache_k, cache_v, lam, sub_norm, pps):
    b, n_pages = page_table.shape
    n_pool, page = cache_k.shape[0], cache_k.shape[1]
    assert n_pages % pps == 0
    ck = cache_k.reshape(n_pool, page, W_A)
    cv = cache_v.reshape(n_pool, page, W_A)
    pt = page_table.reshape(-1)

    def page_spec(i):
        return pl.BlockSpec((None, page, W_A), lambda b, g, pt: (pt[b * n_pages + g * pps + i], 0, 0))

    tok = pl.BlockSpec((1, 1, W_A), lambda b, g, pt: (b, 0, 0))
    grid_spec = pltpu.PrefetchScalarGridSpec(
        num_scalar_prefetch=1,
        grid=(b, n_pages // pps),
        in_specs=[tok, tok, tok,
                  pl.BlockSpec((4, D_HEAD), lambda b, g, pt: (0, 0)),
                  pl.BlockSpec((1, LANES), lambda b, g, pt: (0, 0))]
                 + [page_spec(i) for i in range(pps)] + [page_spec(i) for i in range(pps)],
        out_specs=tok,
        scratch_shapes=[pltpu.VMEM((2 * H_A, 1), F32), pltpu.VMEM((2 * H_A, 1), F32),
                        pltpu.VMEM((2 * H_A, W_A), F32)],
    )
    return pl.pallas_call(
        functools.partial(_decode_kernel, pps=pps),
        grid_spec=grid_spec,
        out_shape=jax.ShapeDtypeStruct((b, 1, W_A), BF16),
        compiler_params=_params(2),
        name="decode_attn",
    )(pt, q, k_new, v_new, lam, sub_norm, *([ck] * pps), *([cv] * pps))


def _mlstm_step_kernel(u_ref, cs_ref, bv_ref, gc_ref, cw_ref, cb_ref, c0_ref, n0_ref, m0_ref,
                       hb_ref, c_ref, n_ref, m_ref, cso_ref, *, nb):
    u = u_ref[...]
    cs = cs_ref[...]
    conv = cb_ref[...] + u * cw_ref[CONV_W - 1:CONV_W, :]
    for j in range(CONV_W - 1):
        conv = conv + cs[:, j * W_U:(j + 1) * W_U] * cw_ref[j:j + 1, :]
    cso_ref[...] = jnp.concatenate([cs[:, W_U:], u], axis=1)
    qk = conv * _sigmoid(conv)
    gc = gc_ref[...]
    bv = bv_ref[...]
    m0_all = m0_ref[...]
    ident = jnp.where(lax.broadcasted_iota(jnp.int32, (DV, DV), 0) == lax.broadcasted_iota(jnp.int32, (DV, DV), 1),
                      1.0, 0.0).astype(BF16)
    m_cols = []
    for h in range(H_B):
        qh = qk[:, h * DQK:(h + 1) * DQK]
        kh = qk[:, H_B * DQK + h * DQK:H_B * DQK + (h + 1) * DQK] * (DQK ** -0.5)
        vh = bv[:, h * DV:(h + 1) * DV]
        ig = gc[:, h:h + 1]
        logf = gc[:, H_B + h:H_B + h + 1]
        m0 = m0_all[:, h:h + 1]
        n0 = n0_ref[:, h, :]
        b_in = logf + m0
        m = jnp.maximum(b_in, ig)
        wts = jnp.exp(ig - m)
        inter = jnp.exp(b_in - m)
        s = jnp.sum(qh * kh, axis=1, keepdims=True) * wts
        qhb = qh.astype(BF16)
        cq_rows = []
        for b in range(nb):
            cq_rows.append(_nt(qhb, c0_ref[b, h].astype(BF16))[b:b + 1, :])
        cq = jnp.concatenate(cq_rows, axis=0)
        num = s * vh + inter * cq
        den = s + inter * jnp.sum(n0 * qh, axis=1, keepdims=True)
        hout = num / jnp.maximum(jnp.abs(den), jnp.exp(-m))
        hb_ref[:, h * DV:(h + 1) * DV] = _rms(hout)
        vcols = _nt(ident, (vh * wts).astype(BF16))
        for b in range(nb):
            c_ref[b, h] = inter[b:b + 1, :] * c0_ref[b, h] + vcols[:, b:b + 1] * kh[b:b + 1, :]
        n_ref[:, h, :] = inter * n0 + wts * kh
        m_cols.append(m)
    m_ref[...] = jnp.concatenate(m_cols, axis=1)


def _mlstm_step(u, conv_state, bv, gc, conv_w, conv_b, c0, n0, m0, nb):
    b = u.shape[0]
    assert b % nb == 0
    row = lambda w: pl.BlockSpec((nb, w), lambda i: (i, 0))
    outs = [jax.ShapeDtypeStruct((b, W_BV), F32),
            jax.ShapeDtypeStruct((b, H_B, DV, DQK), F32),
            jax.ShapeDtypeStruct((b, H_B, DQK), F32),
            jax.ShapeDtypeStruct((b, H_B), F32),
            jax.ShapeDtypeStruct((b, (CONV_W - 1) * W_U), F32)]
    cspec = pl.BlockSpec((nb, H_B, DV, DQK), lambda i: (i, 0, 0, 0))
    nspec = pl.BlockSpec((nb, H_B, DQK), lambda i: (i, 0, 0))
    return pl.pallas_call(
        functools.partial(_mlstm_step_kernel, nb=nb),
        grid=(b // nb,),
        in_specs=[row(W_U), row((CONV_W - 1) * W_U), row(W_BV), row(LANES),
                  pl.BlockSpec((CONV_W, W_U), lambda i: (0, 0)),
                  pl.BlockSpec((1, W_U), lambda i: (0, 0)),
                  cspec, nspec, row(H_B)],
        out_specs=[row(W_BV), cspec, nspec, row(H_B), row((CONV_W - 1) * W_U)],
        out_shape=outs,
        compiler_params=_params(1),
        name="mlstm_step",
    )(u, conv_state, bv, gc, conv_w, conv_b, c0, n0, m0)


def _rope_tables(pos):
    half = D_HEAD // 2
    inv = 1.0 / (ROPE_THETA ** (jnp.arange(half, dtype=F32) / half))
    ang = pos.astype(F32)[:, None] * inv[None, :]
    cos = jnp.cos(ang)
    sin = jnp.sin(ang)
    return jnp.tile(cos, (1, 4)), jnp.tile(jnp.concatenate([-sin, sin], axis=1), (1, 2))


def kernel(x_prompt, x_sample, cache_k, cache_v, page_table, state_C, state_n, state_m, state_conv, meta_tokens, ffn1_norm, ffn1_w_in, ffn1_w_out, mix_norm, w_in, b_if, b_gate, q_norm, k_norm, lam_q1, lam_k1, lam_q2, lam_k2, sub_norm, conv_w, conv_b, mh_norm, p_a, p_b, w_o, ffn2_norm, ffn2_w_in, ffn2_w_out):
    bp, s, d = x_prompt.shape
    bs = x_sample.shape[0]
    assert x_sample.shape[1] == 1 and w_in.shape[0] == 1
    n_pages = page_table.shape[1]
    page = cache_k.shape[2]

    w = w_in[0]
    o = 0
    cols = {}
    for name, width in (("q", W_A), ("k", W_A), ("v", W_A), ("u", W_U), ("bv", W_BV), ("bo", W_BV),
                        ("if", 2 * H_B), ("g", 2 * d)):
        cols[name] = w[:, o:o + width]
        o += width
    gidx = jnp.arange(LANES) // D_HEAD
    wp = dict(
        wqk=jnp.concatenate([cols["q"], cols["k"]], axis=1).astype(BF16),
        wv=cols["v"].astype(BF16), wu=cols["u"].astype(BF16), wbv=cols["bv"].astype(BF16),
        wif=jnp.pad(cols["if"], ((0, 0), (0, LANES - 2 * H_B))).astype(BF16),
        bif=jnp.pad(b_if[0], (0, LANES - 2 * H_B)).reshape(1, LANES),
        qn=jnp.tile(q_norm[0], LANES // D_HEAD).reshape(1, LANES),
        kn=jnp.tile(k_norm[0], LANES // D_HEAD).reshape(1, LANES),
        gmat=jnp.where(gidx[:, None] == gidx[None, :], 1.0 / D_HEAD, 0.0).astype(BF16),
        wg=cols["g"].astype(BF16), bg=b_gate[0].reshape(1, 2 * d), wbo=cols["bo"].astype(BF16),
        mh=mh_norm[0].reshape(1, W_BV), pa=p_a[0].astype(BF16), pb=p_b[0].astype(BF16),
        wo=w_o[0].astype(BF16),
    )
    f1 = (ffn1_norm[0].reshape(1, d), ffn1_w_in[0].astype(BF16), ffn1_w_out[0].astype(BF16))
    f2 = (ffn2_norm[0].reshape(1, d), ffn2_w_in[0].astype(BF16), ffn2_w_out[0].astype(BF16))
    mixn = mix_norm[0].reshape(1, d)
    lam = jnp.stack([lam_q1[0], lam_k1[0], lam_q2[0], lam_k2[0]])
    subn = sub_norm[0].reshape(1, 2 * D_HEAD)
    cw, cb = conv_w[0], conv_b[0].reshape(1, W_U)

    tm = 512 if (bp * s) % 512 == 0 else 256
    xr = x_prompt.reshape(bp * s, d)
    n_small = -(-(bs + N_META) // 256) * 256
    xs = jnp.concatenate([x_sample.reshape(bs, d), meta_tokens.astype(F32),
                          jnp.zeros((n_small - bs - N_META, d), F32)], axis=0)
    past = n_pages * page
    pos_small = jnp.concatenate([jnp.full((bs,), past, jnp.int32), jnp.arange(N_META, dtype=jnp.int32),
                                 jnp.zeros((n_small - bs - N_META,), jnp.int32)])
    cos_r, sin_r = _rope_tables(N_META + jnp.arange(s))
    cos_s, sin_s = _rope_tables(pos_small)

    x1r = _ffn(xr, *f1, tm)
    x1s = _ffn(xs, *f1, n_small)
    q_r, k_r, kb_r, v_r, vb_r, u_r, bv_r, gc_r, gr_r = _proj(x1r, mixn, wp, cos_r, sin_r, tm, s // tm)
    q_s, k_s, _, v_s, _, u_s, bv_s, gc_s, gr_s = _proj(x1s, mixn, wp, cos_s, sin_s, n_small, 1)

    msl = slice(bs, bs + N_META)
    k_meta, v_meta = k_s[msl], v_s[msl]
    pad_meta = lambda a: jnp.pad(a, ((0, LANES - N_META), (0, 0))).astype(BF16)
    _, c_m, n_m, m_m, t_m = _mlstm(
        u_s[msl][None], bv_s[msl][None], gc_s[msl][None], gr_s[:, msl], cw, cb,
        jnp.zeros((1, H_B, DV, LANES), F32), jnp.zeros((1, H_B, LANES), F32),
        jnp.full((1, H_B, LANES), NEG_INF, F32), jnp.zeros((1, 8, W_U), F32), N_META)

    tq = 256
    ya_r = _prompt_attn(q_r.reshape(bp, s, W_A), kb_r.reshape(bp, s, W_A), vb_r.reshape(bp, s, W_A),
                        pad_meta(k_meta), pad_meta(v_meta), lam, subn, tq)
    rep = lambda a: jnp.broadcast_to(a, (bp,) + a.shape[1:])
    hb_r, c_p, n_p, m_p, _ = _mlstm(
        u_r.reshape(bp, s, W_U), bv_r.reshape(bp, s, W_BV), gc_r.reshape(bp, s, LANES), gr_r, cw, cb,
        rep(c_m), rep(n_m), rep(m_m), rep(t_m), CHUNK)

    ssl = slice(0, bs)
    pps = 16 if n_pages % 16 == 0 else n_pages
    ya_s = _decode_attn(page_table, q_s[ssl].reshape(bs, 1, W_A), k_s[ssl].reshape(bs, 1, W_A),
                        v_s[ssl].reshape(bs, 1, W_A), cache_k[0], cache_v[0], lam, subn, pps)
    hb_s, c_s, n_s, m_s, conv_s = _mlstm_step(
        u_s[ssl], state_conv[0].reshape(bs, (CONV_W - 1) * W_U), bv_s[ssl], gc_s[ssl], cw, cb,
        state_C[0], state_n[0], state_m[0], 8)

    y_r = _ffn(_mixer(x1r, ya_r.reshape(bp * s, W_A), hb_r.reshape(bp * s, W_BV), mixn, wp, tm), *f2, tm)
    y_s = _ffn(_mixer(x1s[ssl], ya_s.reshape(bs, W_A), hb_s, mixn, wp, bs), *f2, bs)

    def with_meta(meta, real):
        return jnp.concatenate([jnp.broadcast_to(meta[None], (bp, N_META, W_A)), real.reshape(bp, s, W_A)], axis=1)

    def unpad_heads(a):
        return jnp.stack([a[:, h, ..., (h % 2) * DQK:(h % 2 + 1) * DQK] for h in range(H_B)], axis=1)

    k_prompt = with_meta(k_meta, k_r).reshape(1, bp, N_META + s, 2 * H_A, D_HEAD)
    v_prompt = with_meta(v_meta, v_r).reshape(1, bp, N_META + s, H_A, 2 * D_HEAD)
    return (y_r.reshape(bp, s, d), y_s.reshape(bs, 1, d), k_prompt, v_prompt,
            k_s[ssl].reshape(1, bs, 1, 2 * H_A, D_HEAD), v_s[ssl].reshape(1, bs, 1, H_A, 2 * D_HEAD),
            unpad_heads(c_p)[None], unpad_heads(n_p[:, :, None, :])[:, :, 0][None], m_p[:, :, 0][None],
            u_r.reshape(bp, s, W_U)[:, s - (CONV_W - 1):][None],
            c_s[None], n_s[None], m_s[None], conv_s.reshape(1, bs, CONV_W - 1, W_U))
```

```python
import functools

import jax
import jax.numpy as jnp
from jax import lax
from jax.experimental import pallas as pl
from jax.experimental.pallas import tpu as pltpu

F32 = jnp.float32
BF16 = jnp.bfloat16

N_META = 16
H_A = 4
D_HEAD = 64
H_B = 4
DQK = 64
DV = 128
CONV_W = 4
CHUNK = 128
ROPE_THETA = 10000.0
EPS = 1e-6
ATTN_SCALE = D_HEAD ** -0.5
Q_SCALE = ATTN_SCALE * 1.4426950408889634
LAM_INIT = 0.2
LANES = 128
W_A = 2 * H_A * D_HEAD
W_U = 2 * H_B * DQK
W_BV = H_B * DV
VMEM_LIMIT = 56 * 1024 * 1024
NEG_INF = float("-inf")


def _dot(a, b):
    return jnp.dot(a, b, preferred_element_type=F32)


def _nt(a, b):
    return lax.dot_general(a, b, (((1,), (1,)), ((), ())), preferred_element_type=F32)


def _rms(x):
    return x * lax.rsqrt(jnp.mean(x * x, axis=-1, keepdims=True) + EPS)


def _sigmoid(x):
    return 1.0 / (1.0 + jnp.exp(-x))


def _log_sigmoid(x):
    return jnp.minimum(x, 0.0) - jnp.log1p(jnp.exp(-jnp.abs(x)))


def _split3(x):
    x1 = x.astype(BF16)
    r = x - x1.astype(F32)
    x2 = r.astype(BF16)
    x3 = (r - x2.astype(F32)).astype(BF16)
    return x1, x2, x3


def _params(n_axes):
    return pltpu.CompilerParams(dimension_semantics=("arbitrary",) * n_axes,
                                vmem_limit_bytes=VMEM_LIMIT)


def _const_spec(shape):
    nd = len(shape)
    return pl.BlockSpec(shape, lambda *_: (0,) * nd, pipeline_mode=pl.Buffered(1))


def _ffn_kernel(x_ref, g_ref, win_ref, wout_ref, o_ref, *, d_ff, ck):
    x = x_ref[...]
    h = (_rms(x) * g_ref[...]).astype(BF16)
    acc = jnp.zeros(x.shape, F32)
    for c in range(d_ff // ck):
        g = _dot(h, win_ref[:, c * ck:(c + 1) * ck])
        u = _dot(h, win_ref[:, d_ff + c * ck:d_ff + (c + 1) * ck])
        a = (g * _sigmoid(g) * u).astype(BF16)
        acc = acc + _dot(a, wout_ref[c * ck:(c + 1) * ck, :])
    o_ref[...] = x + 0.5 * acc


def _ffn(x, norm_g, w_in, w_out, tm):
    m, d = x.shape
    d_ff = w_out.shape[0]
    ck = 256
    assert m % tm == 0 and d_ff % ck == 0
    return pl.pallas_call(
        functools.partial(_ffn_kernel, d_ff=d_ff, ck=ck),
        grid=(m // tm,),
        in_specs=[pl.BlockSpec((tm, d), lambda i: (i, 0)),
                  _const_spec((1, d)),
                  _const_spec((d, 2 * d_ff)),
                  _const_spec((d_ff, d))],
        out_specs=pl.BlockSpec((tm, d), lambda i: (i, 0)),
        out_shape=jax.ShapeDtypeStruct((m, d), F32),
        compiler_params=_params(1),
        name="ffn",
    )(x, norm_g, w_in, w_out)


def _proj_kernel(x_ref, g_ref, wqk_ref, wv_ref, wu_ref, wbv_ref, wif_ref, bif_ref,
                 qn_ref, kn_ref, cos_ref, sin_ref, gmat_ref,
                 q_ref, k_ref, kb_ref, v_ref, vb_ref, u_ref, bv_ref, gc_ref, gr_ref):
    x = x_ref[...]
    tm = x.shape[0]
    h = (_rms(x) * g_ref[...]).astype(BF16)
    cos = cos_ref[...]
    sin = sin_ref[...]
    gmat = gmat_ref[...]
    lane = lax.broadcasted_iota(jnp.int32, (tm, LANES), 1)
    first_half = (lane % D_HEAD) < (D_HEAD // 2)

    def norm_rope(z, nw):
        sq = z * z
        hi = sq.astype(BF16)
        lo = (sq - hi.astype(F32)).astype(BF16)
        ms = _dot(hi, gmat) + _dot(lo, gmat)
        y = z * lax.rsqrt(ms + EPS) * nw
        rot = jnp.where(first_half, pltpu.roll(y, LANES - D_HEAD // 2, 1), pltpu.roll(y, D_HEAD // 2, 1))
        return y * cos + rot * sin

    qn = qn_ref[...]
    kn = kn_ref[...]
    for s in range(W_A // LANES):
        sl = slice(s * LANES, (s + 1) * LANES)
        zq = _dot(h, wqk_ref[:, s * LANES:(s + 1) * LANES])
        q_ref[:, sl] = (norm_rope(zq, qn) * Q_SCALE).astype(BF16)
        zk = _dot(h, wqk_ref[:, W_A + s * LANES:W_A + (s + 1) * LANES])
        kk = norm_rope(zk, kn)
        k_ref[:, sl] = kk
        kb_ref[:, sl] = kk.astype(BF16)
    v = _dot(h, wv_ref[...])
    v_ref[...] = v
    vb_ref[...] = v.astype(BF16)
    u_ref[...] = _dot(h, wu_ref[...])
    bv_ref[...] = _dot(h, wbv_ref[...])
    z = _dot(h, wif_ref[...]) + bif_ref[...]
    gc = jnp.where(lane < H_B, z, _log_sigmoid(z))
    gc_ref[...] = gc
    gr_ref[...] = gc.T[:8, :]


def _proj(x, mix_norm, wp, cos_t, sin_t, tm, pos_blocks):
    m, d = x.shape
    assert m % tm == 0
    row = lambda w: pl.BlockSpec((tm, w), lambda i: (i, 0))
    tab = pl.BlockSpec((tm, LANES), lambda i: (i % pos_blocks, 0))
    outs = [
        jax.ShapeDtypeStruct((m, W_A), BF16),
        jax.ShapeDtypeStruct((m, W_A), F32),
        jax.ShapeDtypeStruct((m, W_A), BF16),
        jax.ShapeDtypeStruct((m, W_A), F32),
        jax.ShapeDtypeStruct((m, W_A), BF16),
        jax.ShapeDtypeStruct((m, W_U), F32),
        jax.ShapeDtypeStruct((m, W_BV), F32),
        jax.ShapeDtypeStruct((m, LANES), F32),
        jax.ShapeDtypeStruct((8, m), F32),
    ]
    return pl.pallas_call(
        _proj_kernel,
        grid=(m // tm,),
        in_specs=[row(d), _const_spec((1, d)),
                  _const_spec((d, 2 * W_A)), _const_spec((d, W_A)), _const_spec((d, W_U)),
                  _const_spec((d, W_BV)), _const_spec((d, LANES)), _const_spec((1, LANES)),
                  _const_spec((1, LANES)), _const_spec((1, LANES)), tab, tab,
                  _const_spec((LANES, LANES))],
        out_specs=[row(W_A), row(W_A), row(W_A), row(W_A), row(W_A), row(W_U), row(W_BV),
                   row(LANES), pl.BlockSpec((8, tm), lambda i: (0, i))],
        out_shape=outs,
        compiler_params=_params(1),
        name="proj",
    )(x, mix_norm, wp["wqk"], wp["wv"], wp["wu"], wp["wbv"], wp["wif"], wp["bif"],
      wp["qn"], wp["kn"], cos_t, sin_t, wp["gmat"])


def _lambda(lam_ref):
    a = jnp.sum(lam_ref[0:1, :] * lam_ref[1:2, :], axis=-1, keepdims=True)
    b = jnp.sum(lam_ref[2:3, :] * lam_ref[3:4, :], axis=-1, keepdims=True)
    return jnp.exp(a) - jnp.exp(b) + LAM_INIT


def _attn_kernel(q_ref, k_ref, v_ref, km_ref, vm_ref, lam_ref, sn_ref, o_ref,
                 qq_s, s0_s, s1_s, m_s, acc_s, *, tq):
    i = pl.program_id(2)
    q = q_ref[0]
    lane = lax.broadcasted_iota(jnp.int32, (tq, LANES), 1)
    zero = jnp.zeros_like(q)
    qq_s[0:tq, :] = jnp.where(lane < D_HEAD, q, zero)
    qq_s[tq:2 * tq, :] = jnp.where(lane >= D_HEAD, q, zero)

    def with_ones(v):
        return jnp.concatenate([v, jnp.ones_like(v)], axis=1)

    def scores(j, s_ref):
        off = pl.multiple_of(j * tq, tq)
        s_ref[...] = _nt(qq_s[...], k_ref[0, pl.ds(off, tq), :])

    def consume(j, s_ref, masked):
        off = pl.multiple_of(j * tq, tq)
        s = s_ref[...]
        if masked:
            row = lax.broadcasted_iota(jnp.int32, s.shape, 0)
            colk = lax.broadcasted_iota(jnp.int32, s.shape, 1)
            qidx = jnp.where(row >= tq, row - tq, row)
            s = jnp.where(colk <= qidx, s, NEG_INF)
        m = m_s[...]
        m_new = jnp.maximum(m, jnp.max(s, axis=1, keepdims=True))
        p = jnp.exp2(s - m_new).astype(BF16)
        acc_s[...] = jnp.exp2(m - m_new) * acc_s[...] + _dot(p, with_ones(v_ref[0, pl.ds(off, tq), :]))
        m_s[...] = m_new

    s = _nt(qq_s[...], km_ref[...])
    col = lax.broadcasted_iota(jnp.int32, s.shape, 1)
    s = jnp.where(col < N_META, s, NEG_INF)
    m = jnp.max(s, axis=1, keepdims=True)
    m_s[...] = m
    acc_s[...] = _dot(jnp.exp2(s - m).astype(BF16), with_ones(vm_ref[...]))

    scores(0, s0_s)

    def pair(t, _):
        scores(2 * t + 1, s1_s)
        consume(2 * t, s0_s, False)
        scores(2 * t + 2, s0_s)
        consume(2 * t + 1, s1_s, False)
        return 0

    lax.fori_loop(0, i // 2, pair, 0)

    @pl.when(i % 2 == 1)
    def _():
        scores(i, s1_s)
        consume(i - 1, s0_s, False)
        consume(i, s1_s, True)

    @pl.when(i % 2 == 0)
    def _():
        consume(i, s0_s, True)

    acc = acc_s[...]
    o = acc[:, :LANES] / acc[:, LANES:]
    oa = o[:tq] - _lambda(lam_ref) * o[tq:]
    o_ref[0] = (_rms(oa) * sn_ref[...] * (1.0 - LAM_INIT)).astype(BF16)


def _prompt_attn(q, kb, vb, kmeta, vmeta, lam, sub_norm, tq):
    b, s, _ = q.shape
    assert s % tq == 0
    return pl.pallas_call(
        functools.partial(_attn_kernel, tq=tq),
        grid=(b, H_A, s // tq),
        in_specs=[pl.BlockSpec((1, tq, LANES), lambda b, h, i: (b, i, h)),
                  pl.BlockSpec((1, s, LANES), lambda b, h, i: (b, 0, h)),
                  pl.BlockSpec((1, s, LANES), lambda b, h, i: (b, 0, h)),
                  pl.BlockSpec((LANES, LANES), lambda b, h, i: (0, h)),
                  pl.BlockSpec((LANES, LANES), lambda b, h, i: (0, h)),
                  pl.BlockSpec((4, D_HEAD), lambda b, h, i: (0, 0)),
                  pl.BlockSpec((1, LANES), lambda b, h, i: (0, 0))],
        out_specs=pl.BlockSpec((1, tq, LANES), lambda b, h, i: (b, i, h)),
        out_shape=jax.ShapeDtypeStruct((b, s, W_A), BF16),
        scratch_shapes=[pltpu.VMEM((2 * tq, LANES), BF16),
                        pltpu.VMEM((2 * tq, tq), F32), pltpu.VMEM((2 * tq, tq), F32),
                        pltpu.VMEM((2 * tq, 1), F32), pltpu.VMEM((2 * tq, 2 * LANES), F32)],
        compiler_params=_params(3),
        name="prompt_attn",
    )(q, kb, vb, kmeta, vmeta, lam, sub_norm)


def _mlstm_kernel(u_ref, bv_ref, gc_ref, gr_ref, cw_ref, cb_ref, c0_ref, n0_ref, m0_ref, t0_ref,
                  hb_ref, c_ref, n_ref, m_ref, t_ref, ue_ref, *, el):
    c = pl.program_id(1)

    @pl.when(c == 0)
    def _():
        c_ref[...] = c0_ref[...]
        n_ref[...] = n0_ref[...]
        m_ref[...] = m0_ref[...]
        t_ref[...] = t0_ref[...]

    ue_ref[0:8, :] = t_ref[0]
    ue_ref[8:8 + el, :] = u_ref[0]
    conv = cb_ref[...] + ue_ref[5:5 + el, :] * cw_ref[0:1, :]
    for j in range(1, CONV_W):
        conv = conv + ue_ref[5 + j:5 + j + el, :] * cw_ref[j:j + 1, :]
    t_ref[0] = ue_ref[el:el + 8, :]
    qk = conv * _sigmoid(conv)

    gc = gc_ref[0]
    gr = gr_ref[...]
    ri = lax.broadcasted_iota(jnp.int32, (el, el), 0)
    ci = lax.broadcasted_iota(jnp.int32, (el, el), 1)
    causal = ci <= ri
    tri = jnp.where(causal, 1.0, 0.0).astype(BF16)
    tri_t = jnp.where(ri <= ci, 1.0, 0.0).astype(BF16)
    g1, g2, g3 = _split3(gc)
    fh_cols = _dot(tri, g1) + _dot(tri, g2) + _dot(tri, g3)
    r1, r2, r3 = _split3(gr)
    fh_rows = _dot(r1, tri_t) + _dot(r2, tri_t) + _dot(r3, tri_t)

    lane = lax.broadcasted_iota(jnp.int32, (el, LANES), 1)
    for h in range(H_B):
        pair = h // 2
        mine = (lane >= DQK) if (h % 2) else (lane < DQK)
        qp = qk[:, pair * LANES:(pair + 1) * LANES]
        kp = qk[:, H_B * DQK + pair * LANES:H_B * DQK + (pair + 1) * LANES] * (DQK ** -0.5)
        km = jnp.where(mine, kp, 0.0)
        qm = jnp.where(mine, qp, 0.0)
        vh = bv_ref[0, :, h * DV:(h + 1) * DV]
        fhc = fh_cols[:, H_B + h:H_B + h + 1]
        fhr = fh_rows[H_B + h:H_B + h + 1, :]
        igc = gc[:, h:h + 1]
        igr = gr[h:h + 1, :]
        m0 = m_ref[0, h:h + 1, 0:1]
        c0 = c_ref[0, h]
        n0 = n_ref[0, h:h + 1, :]

        dmat = jnp.where(causal, fhc - fhr + igr, NEG_INF)
        b_in = fhc + m0
        m = jnp.maximum(b_in, jnp.max(dmat, axis=1, keepdims=True))
        wts = jnp.exp(dmat - m)
        inter = jnp.exp(b_in - m)
        qmb = qm.astype(BF16)
        s = _nt(qmb, kp.astype(BF16)) * wts
        num = _dot(s.astype(BF16), vh.astype(BF16)) + inter * _nt(qmb, c0.astype(BF16))
        den = jnp.sum(s, axis=1, keepdims=True) + inter * jnp.sum(qm * n0, axis=1, keepdims=True)
        hout = num / jnp.maximum(jnp.abs(den), jnp.exp(-m))
        hb_ref[0, :, h * DV:(h + 1) * DV] = _rms(hout)

        m_new = m[el - 1:el, :]
        fl = fhc[el - 1:el, :]
        wl = jnp.exp(fl - fhc + igc - m_new)
        decay = jnp.exp(fl + m0 - m_new)
        vw = (vh * wl).T.astype(BF16)
        c_ref[0, h] = decay * c0 + _dot(vw, km.astype(BF16))
        n_ref[0, h:h + 1, :] = decay * n0 + jnp.sum(wl * km, axis=0, keepdims=True)
        m_ref[0, h:h + 1, :] = jnp.broadcast_to(m_new, (1, LANES))


def _mlstm(u, bv, gc, gr, conv_w, conv_b, c0, n0, m0, t0, el):
    b, s, _ = u.shape
    nc = s // el
    assert s % el == 0 and gr.shape == (8, b * s)
    st = lambda shape: pl.BlockSpec((1,) + shape, lambda b, c: (b,) + (0,) * len(shape))
    seq = lambda w: pl.BlockSpec((1, el, w), lambda b, c: (b, c, 0))
    outs = [jax.ShapeDtypeStruct((b, s, W_BV), F32),
            jax.ShapeDtypeStruct((b, H_B, DV, LANES), F32),
            jax.ShapeDtypeStruct((b, H_B, LANES), F32),
            jax.ShapeDtypeStruct((b, H_B, LANES), F32),
            jax.ShapeDtypeStruct((b, 8, W_U), F32)]
    return pl.pallas_call(
        functools.partial(_mlstm_kernel, el=el),
        grid=(b, nc),
        in_specs=[seq(W_U), seq(W_BV), seq(LANES),
                  pl.BlockSpec((8, el), lambda b, c: (0, b * nc + c)),
                  pl.BlockSpec((CONV_W, W_U), lambda b, c: (0, 0)),
                  pl.BlockSpec((1, W_U), lambda b, c: (0, 0)),
                  st((H_B, DV, LANES)), st((H_B, LANES)), st((H_B, LANES)), st((8, W_U))],
        out_specs=[seq(W_BV), st((H_B, DV, LANES)), st((H_B, LANES)), st((H_B, LANES)), st((8, W_U))],
        out_shape=outs,
        scratch_shapes=[pltpu.VMEM((el + 8, W_U), F32)],
        compiler_params=_params(2),
        name="mlstm_scan",
    )(u, bv, gc, gr, conv_w, conv_b, c0, n0, m0, t0)


def _mixer_kernel(x_ref, ya_ref, hb_ref, g_ref, wg_ref, bg_ref, wbo_ref, mh_ref, pa_ref, pb_ref, wo_ref, o_ref):
    x = x_ref[...]
    d = x.shape[1]
    h = (_rms(x) * g_ref[...]).astype(BF16)
    bo = _dot(h, wbo_ref[...])
    yb = (hb_ref[...] * mh_ref[...] * _sigmoid(bo)).astype(BF16)
    pa = _dot(ya_ref[...], pa_ref[...])
    pb = _dot(yb, pb_ref[...])
    ga = _sigmoid(_dot(h, wg_ref[:, :d]) + bg_ref[:, :d])
    gb = _sigmoid(_dot(h, wg_ref[:, d:]) + bg_ref[:, d:])
    merged = (ga * pa + gb * pb).astype(BF16)
    o_ref[...] = x + _dot(merged, wo_ref[...])


def _mixer(x, ya, hbn, mix_norm, wp, tm):
    m, d = x.shape
    assert m % tm == 0
    row = lambda w: pl.BlockSpec((tm, w), lambda i: (i, 0))
    return pl.pallas_call(
        _mixer_kernel,
        grid=(m // tm,),
        in_specs=[row(d), row(W_A), row(W_BV), _const_spec((1, d)),
                  _const_spec((d, 2 * d)), _const_spec((1, 2 * d)), _const_spec((d, W_BV)),
                  _const_spec((1, W_BV)), _const_spec((W_A, d)), _const_spec((W_BV, d)),
                  _const_spec((d, d))],
        out_specs=row(d),
        out_shape=jax.ShapeDtypeStruct((m, d), F32),
        compiler_params=_params(1),
        name="mixer_out",
    )(x, ya, hbn, mix_norm, wp["wg"], wp["bg"], wp["wbo"], wp["mh"], wp["pa"], wp["pb"], wp["wo"])


def _decode_kernel(pt_ref, qrow_ref, q_ref, kn_ref, vn_ref, lam_ref, sn_ref, *refs, pps):
    k_refs = refs[:pps]
    v_refs = refs[pps:2 * pps]
    o_ref, m_s, l_s, acc_s = refs[2 * pps:]
    g = pl.program_id(1)
    ng = pl.num_programs(1)
    nsub = 2 * H_A
    page = k_refs[0].shape[1]

    @pl.when(g == 0)
    def _():
        m_s[...] = jnp.full(m_s.shape, NEG_INF, F32)
        l_s[...] = jnp.zeros(l_s.shape, F32)
        acc_s[...] = jnp.zeros(acc_s.shape, F32)

    row = lax.broadcasted_iota(jnp.int32, (nsub, W_A), 0)
    lane = lax.broadcasted_iota(jnp.int32, (nsub, W_A), 1)
    qbd = jnp.where(lane // D_HEAD == row, jnp.broadcast_to(qrow_ref[0].astype(F32), (nsub, W_A)), 0.0).astype(BF16)

    s = jnp.concatenate([_dot(qbd, k_refs[i][...].astype(BF16)) for i in range(pps)], axis=1)
    m = m_s[...]
    m_new = jnp.maximum(m, jnp.max(s, axis=1, keepdims=True))
    alpha = jnp.exp2(m - m_new)
    p = jnp.exp2(s - m_new)
    l_s[...] = alpha * l_s[...] + jnp.sum(p, axis=1, keepdims=True)
    m_s[...] = m_new
    pb = p.astype(BF16)
    for h in range(H_A):
        pv = _dot(pb[:, 0:page], v_refs[0][pl.ds(h, page, stride=H_A), :].astype(BF16))
        for i in range(1, pps):
            pv = pv + _dot(pb[:, i * page:(i + 1) * page], v_refs[i][pl.ds(h, page, stride=H_A), :].astype(BF16))
        acc_s[h] = alpha * acc_s[h] + pv

    @pl.when(g == ng - 1)
    def _():
        q8 = q_ref[0].astype(F32)
        s_self = jnp.sum(q8 * kn_ref[0].astype(BF16).astype(F32), axis=1, keepdims=True)
        m = m_s[...]
        m_new = jnp.maximum(m, s_self)
        alpha = jnp.exp2(m - m_new)
        p_self = jnp.exp2(s_self - m_new)
        l = alpha * l_s[...] + p_self
        lam = _lambda(lam_ref)
        vn = vn_ref[0].astype(BF16).astype(F32)
        outs = []
        for h in range(H_A):
            o = (alpha * acc_s[h] + p_self * vn[h:h + 1, :]) / l
            oa = o[2 * h:2 * h + 1, :] - lam * o[2 * h + 1:2 * h + 2, :]
            outs.append(_rms(oa) * sn_ref[...] * (1.0 - LAM_INIT))
        o_ref[0] = jnp.concatenate(outs, axis=1).astype(BF16)


def _decode_attn(page_table, q, k_new, v_new, cache_k, cache_v, lam, sub_norm, pps):
    b, n_pages = page_table.shape
    n_pool, page = cache_k.shape[1], cache_k.shape[2]
    nsub = 2 * H_A
    assert n_pages % pps == 0 and cache_k.shape[0] == 1
    pt = page_table.reshape(-1)
    ck = jnp.transpose(cache_k[0], (0, 2, 3, 1)).reshape(n_pool, nsub * D_HEAD, page)
    cv = cache_v[0].reshape(n_pool, page * H_A, 2 * D_HEAD)

    kspec = lambda i: pl.BlockSpec((None, nsub * D_HEAD, page),
                                   lambda b, g, pt: (pt[b * n_pages + g * pps + i], 0, 0))
    vspec = lambda i: pl.BlockSpec((None, page * H_A, 2 * D_HEAD),
                                   lambda b, g, pt: (pt[b * n_pages + g * pps + i], 0, 0))
    tok = lambda heads, width: pl.BlockSpec((1, heads, width), lambda b, g, pt: (b, 0, 0))
    grid_spec = pltpu.PrefetchScalarGridSpec(
        num_scalar_prefetch=1,
        grid=(b, n_pages // pps),
        in_specs=[tok(1, W_A), tok(nsub, D_HEAD), tok(nsub, D_HEAD), tok(H_A, 2 * D_HEAD),
                  pl.BlockSpec((4, D_HEAD), lambda b, g, pt: (0, 0)),
                  pl.BlockSpec((1, LANES), lambda b, g, pt: (0, 0))]
                 + [kspec(i) for i in range(pps)] + [vspec(i) for i in range(pps)],
        out_specs=pl.BlockSpec((1, 1, W_A), lambda b, g, pt: (b, 0, 0)),
        scratch_shapes=[pltpu.VMEM((nsub, 1), F32), pltpu.VMEM((nsub, 1), F32),
                        pltpu.VMEM((H_A, nsub, 2 * D_HEAD), F32)],
    )
    return pl.pallas_call(
        functools.partial(_decode_kernel, pps=pps),
        grid_spec=grid_spec,
        out_shape=jax.ShapeDtypeStruct((b, 1, W_A), BF16),
        compiler_params=_params(2),
        name="decode_attn",
    )(pt, q.reshape(b, 1, W_A), q, k_new, v_new, lam, sub_norm, *([ck] * pps), *([cv] * pps))


def _mlstm_step_kernel(u_ref, cs_ref, bv_ref, gc_ref, cw_ref, cb_ref, c0_ref, n0_ref, m0_ref,
                       hb_ref, c_ref, n_ref, m_ref, cso_ref, *, nb):
    u = u_ref[...]
    cs = cs_ref[...]
    conv = cb_ref[...] + u * cw_ref[CONV_W - 1:CONV_W, :]
    for j in range(CONV_W - 1):
        conv = conv + cs[:, j * W_U:(j + 1) * W_U] * cw_ref[j:j + 1, :]
    cso_ref[...] = jnp.concatenate([cs[:, W_U:], u], axis=1)
    qk = conv * _sigmoid(conv)
    gc = gc_ref[...]
    bv = bv_ref[...]
    m0_all = m0_ref[...]
    ident = jnp.where(lax.broadcasted_iota(jnp.int32, (DV, DV), 0) == lax.broadcasted_iota(jnp.int32, (DV, DV), 1),
                      1.0, 0.0).astype(BF16)
    m_cols = []
    for h in range(H_B):
        qh = qk[:, h * DQK:(h + 1) * DQK]
        kh = qk[:, H_B * DQK + h * DQK:H_B * DQK + (h + 1) * DQK] * (DQK ** -0.5)
        vh = bv[:, h * DV:(h + 1) * DV]
        ig = gc[:, h:h + 1]
        logf = gc[:, H_B + h:H_B + h + 1]
        m0 = m0_all[:, h:h + 1]
        n0 = n0_ref[:, h, :]
        b_in = logf + m0
        m = jnp.maximum(b_in, ig)
        wts = jnp.exp(ig - m)
        inter = jnp.exp(b_in - m)
        s = jnp.sum(qh * kh, axis=1, keepdims=True) * wts
        qhb = qh.astype(BF16)
        cq_rows = []
        for b in range(nb):
            cq_rows.append(_nt(qhb, c0_ref[b, h].astype(BF16))[b:b + 1, :])
        cq = jnp.concatenate(cq_rows, axis=0)
        num = s * vh + inter * cq
        den = s + inter * jnp.sum(n0 * qh, axis=1, keepdims=True)
        hout = num / jnp.maximum(jnp.abs(den), jnp.exp(-m))
        hb_ref[:, h * DV:(h + 1) * DV] = _rms(hout)
        vcols = _nt(ident, (vh * wts).astype(BF16))
        for b in range(nb):
            c_ref[b, h] = inter[b:b + 1, :] * c0_ref[b, h] + vcols[:, b:b + 1] * kh[b:b + 1, :]
        n_ref[:, h, :] = inter * n0 + wts * kh
        m_cols.append(m)
    m_ref[...] = jnp.concatenate(m_cols, axis=1)


def _mlstm_step(u, conv_state, bv, gc, conv_w, conv_b, c0, n0, m0, nb):
    b = u.shape[0]
    assert b % nb == 0
    row = lambda w: pl.BlockSpec((nb, w), lambda i: (i, 0))
    outs = [jax.ShapeDtypeStruct((b, W_BV), F32),
            jax.ShapeDtypeStruct((b, H_B, DV, DQK), F32),
            jax.ShapeDtypeStruct((b, H_B, DQK), F32),
            jax.ShapeDtypeStruct((b, H_B), F32),
            jax.ShapeDtypeStruct((b, (CONV_W - 1) * W_U), F32)]
    cspec = pl.BlockSpec((nb, H_B, DV, DQK), lambda i: (i, 0, 0, 0))
    nspec = pl.BlockSpec((nb, H_B, DQK), lambda i: (i, 0, 0))
    return pl.pallas_call(
        functools.partial(_mlstm_step_kernel, nb=nb),
        grid=(b // nb,),
        in_specs=[row(W_U), row((CONV_W - 1) * W_U), row(W_BV), row(LANES),
                  pl.BlockSpec((CONV_W, W_U), lambda i: (0, 0)),
                  pl.BlockSpec((1, W_U), lambda i: (0, 0)),
                  cspec, nspec, row(H_B)],
        out_specs=[row(W_BV), cspec, nspec, row(H_B), row((CONV_W - 1) * W_U)],
        out_shape=outs,
        compiler_params=_params(1),
        name="mlstm_step",
    )(u, conv_state, bv, gc, conv_w, conv_b, c0, n0, m0)


def _rope_tables(pos):
    half = D_HEAD // 2
    inv = 1.0 / (ROPE_THETA ** (jnp.arange(half, dtype=F32) / half))
    ang = pos.astype(F32)[:, None] * inv[None, :]
    cos = jnp.cos(ang)
    sin = jnp.sin(ang)
    return jnp.tile(cos, (1, 4)), jnp.tile(jnp.concatenate([-sin, sin], axis=1), (1, 2))


def kernel(x_prompt, x_sample, cache_k, cache_v, page_table, state_C, state_n, state_m, state_conv, meta_tokens, ffn1_norm, ffn1_w_in, ffn1_w_out, mix_norm, w_in, b_if, b_gate, q_norm, k_norm, lam_q1, lam_k1, lam_q2, lam_k2, sub_norm, conv_w, conv_b, mh_norm, p_a, p_b, w_o, ffn2_norm, ffn2_w_in, ffn2_w_out):
    bp, s, d = x_prompt.shape
    bs = x_sample.shape[0]
    assert x_sample.shape[1] == 1 and w_in.shape[0] == 1
    n_pages = page_table.shape[1]
    page = cache_k.shape[2]

    w = w_in[0]
    o = 0
    cols = {}
    for name, width in (("q", W_A), ("k", W_A), ("v", W_A), ("u", W_U), ("bv", W_BV), ("bo", W_BV),
                        ("if", 2 * H_B), ("g", 2 * d)):
        cols[name] = w[:, o:o + width]
        o += width
    gidx = jnp.arange(LANES) // D_HEAD
    wp = dict(
        wqk=jnp.concatenate([cols["q"], cols["k"]], axis=1).astype(BF16),
        wv=cols["v"].astype(BF16), wu=cols["u"].astype(BF16), wbv=cols["bv"].astype(BF16),
        wif=jnp.pad(cols["if"], ((0, 0), (0, LANES - 2 * H_B))).astype(BF16),
        bif=jnp.pad(b_if[0], (0, LANES - 2 * H_B)).reshape(1, LANES),
        qn=jnp.tile(q_norm[0], LANES // D_HEAD).reshape(1, LANES),
        kn=jnp.tile(k_norm[0], LANES // D_HEAD).reshape(1, LANES),
        gmat=jnp.where(gidx[:, None] == gidx[None, :], 1.0 / D_HEAD, 0.0).astype(BF16),
        wg=cols["g"].astype(BF16), bg=b_gate[0].reshape(1, 2 * d), wbo=cols["bo"].astype(BF16),
        mh=mh_norm[0].reshape(1, W_BV), pa=p_a[0].astype(BF16), pb=p_b[0].astype(BF16),
        wo=w_o[0].astype(BF16),
    )
    f1 = (ffn1_norm[0].reshape(1, d), ffn1_w_in[0].astype(BF16), ffn1_w_out[0].astype(BF16))
    f2 = (ffn2_norm[0].reshape(1, d), ffn2_w_in[0].astype(BF16), ffn2_w_out[0].astype(BF16))
    mixn = mix_norm[0].reshape(1, d)
    lam = jnp.stack([lam_q1[0], lam_k1[0], lam_q2[0], lam_k2[0]])
    subn = sub_norm[0].reshape(1, 2 * D_HEAD)
    cw, cb = conv_w[0], conv_b[0].reshape(1, W_U)

    tm = 512 if (bp * s) % 512 == 0 else 256
    xr = x_prompt.reshape(bp * s, d)
    n_small = -(-(bs + N_META) // 256) * 256
    xs = jnp.concatenate([x_sample.reshape(bs, d), meta_tokens.astype(F32),
                          jnp.zeros((n_small - bs - N_META, d), F32)], axis=0)
    past = n_pages * page
    pos_small = jnp.concatenate([jnp.full((bs,), past, jnp.int32), jnp.arange(N_META, dtype=jnp.int32),
                                 jnp.zeros((n_small - bs - N_META,), jnp.int32)])
    cos_r, sin_r = _rope_tables(N_META + jnp.arange(s))
    cos_s, sin_s = _rope_tables(pos_small)

    x1r = _ffn(xr, *f1, tm)
    x1s = _ffn(xs, *f1, n_small)
    q_r, k_r, kb_r, v_r, vb_r, u_r, bv_r, gc_r, gr_r = _proj(x1r, mixn, wp, cos_r, sin_r, tm, s // tm)
    q_s, k_s, _, v_s, _, u_s, bv_s, gc_s, gr_s = _proj(x1s, mixn, wp, cos_s, sin_s, n_small, 1)

    msl = slice(bs, bs + N_META)
    k_meta, v_meta = k_s[msl], v_s[msl]
    pad_meta = lambda a: jnp.pad(a, ((0, LANES - N_META), (0, 0))).astype(BF16)
    _, c_m, n_m, m_m, t_m = _mlstm(
        u_s[msl][None], bv_s[msl][None], gc_s[msl][None], gr_s[:, msl], cw, cb,
        jnp.zeros((1, H_B, DV, LANES), F32), jnp.zeros((1, H_B, LANES), F32),
        jnp.full((1, H_B, LANES), NEG_INF, F32), jnp.zeros((1, 8, W_U), F32), N_META)

    tq = 512 if s % 512 == 0 else 256
    ya_r =_prompt_attn(q_r.reshape(bp, s, W_A), kb_r.reshape(bp, s, W_A), vb_r.reshape(bp, s, W_A),
                        pad_meta(k_meta), pad_meta(v_meta), lam, subn, tq)
    rep = lambda a: jnp.broadcast_to(a, (bp,) + a.shape[1:])
    hb_r, c_p, n_p, m_p, _ = _mlstm(
        u_r.reshape(bp, s, W_U), bv_r.reshape(bp, s, W_BV), gc_r.reshape(bp, s, LANES), gr_r, cw, cb,
        rep(c_m), rep(n_m), rep(m_m), rep(t_m), CHUNK)

    ssl = slice(0, bs)
    pps = 16 if n_pages % 16 == 0 else n_pages
    ya_s = _decode_attn(page_table, q_s[ssl].reshape(bs, 2 * H_A, D_HEAD), k_s[ssl].reshape(bs, 2 * H_A, D_HEAD),
                        v_s[ssl].reshape(bs, H_A, 2 * D_HEAD), cache_k, cache_v, lam, subn, pps)
    hb_s, c_s, n_s, m_s, conv_s = _mlstm_step(
        u_s[ssl], state_conv[0].reshape(bs, (CONV_W - 1) * W_U), bv_s[ssl], gc_s[ssl], cw, cb,
        state_C[0], state_n[0], state_m[0], 8)

    y_r = _ffn(_mixer(x1r, ya_r.reshape(bp * s, W_A), hb_r.reshape(bp * s, W_BV), mixn, wp, tm), *f2, tm)
    y_s = _ffn(_mixer(x1s[ssl], ya_s.reshape(bs, W_A), hb_s, mixn, wp, bs), *f2, bs)

    def with_meta(meta, real):
        return jnp.concatenate([jnp.broadcast_to(meta[None], (bp, N_META, W_A)), real.reshape(bp, s, W_A)], axis=1)

    def unpad_heads(a):
        return jnp.stack([a[:, h, ..., (h % 2) * DQK:(h % 2 + 1) * DQK] for h in range(H_B)], axis=1)

    k_prompt = with_meta(k_meta, k_r).reshape(1, bp, N_META + s, 2 * H_A, D_HEAD)
    v_prompt = with_meta(v_meta, v_r).reshape(1, bp, N_META + s, H_A, 2 * D_HEAD)
    return (y_r.reshape(bp, s, d), y_s.reshape(bs, 1, d), k_prompt, v_prompt,
            k_s[ssl].reshape(1, bs, 1, 2 * H_A, D_HEAD), v_s[ssl].reshape(1, bs, 1, H_A, 2 * D_HEAD),
            unpad_heads(c_p)[None], unpad_heads(n_p[:, :, None, :])[:, :, 0][None], m_p[:, :, 0][None],
            u_r.reshape(bp, s, W_U)[:, s - (CONV_W - 1):][None],
            c_s[None], n_s[None], m_s[None], conv_s.reshape(1, bs, CONV_W - 1, W_U))
```

```python
import functools

import jax
import jax.numpy as jnp
from jax import lax
from jax.experimental import pallas as pl
from jax.experimental.pallas import tpu as pltpu

F32 = jnp.float32
BF16 = jnp.bfloat16

N_META = 16
H_A = 4
D_HEAD = 64
H_B = 4
DQK = 64
DV = 128
CONV_W = 4
CHUNK = 128
ROPE_THETA = 10000.0
EPS = 1e-6
ATTN_SCALE = D_HEAD ** -0.5
Q_SCALE = ATTN_SCALE * 1.4426950408889634
LAM_INIT = 0.2
LANES = 128
W_A = 2 * H_A * D_HEAD
W_U = 2 * H_B * DQK
W_BV = H_B * DV
VMEM_LIMIT = 56 * 1024 * 1024
NEG_INF = float("-inf")


def _dot(a, b):
    return jnp.dot(a, b, preferred_element_type=F32)


def _nt(a, b):
    return lax.dot_general(a, b, (((1,), (1,)), ((), ())), preferred_element_type=F32)


def _bdot(a, b, ca, cb):
    return lax.dot_general(a, b, (((ca,), (cb,)), ((0,), (0,))), preferred_element_type=F32)


def _rms(x):
    return x * lax.rsqrt(jnp.mean(x * x, axis=-1, keepdims=True) + EPS)


def _sigmoid(x):
    return 1.0 / (1.0 + jnp.exp(-x))


def _log_sigmoid(x):
    return jnp.minimum(x, 0.0) - jnp.log1p(jnp.exp(-jnp.abs(x)))


def _split3(x):
    x1 = x.astype(BF16)
    r = x - x1.astype(F32)
    x2 = r.astype(BF16)
    x3 = (r - x2.astype(F32)).astype(BF16)
    return x1, x2, x3


def _params(n_axes):
    return pltpu.CompilerParams(dimension_semantics=("arbitrary",) * n_axes,
                                vmem_limit_bytes=VMEM_LIMIT)


def _const_spec(shape):
    nd = len(shape)
    return pl.BlockSpec(shape, lambda *_: (0,) * nd, pipeline_mode=pl.Buffered(1))


def _ffn_kernel(x_ref, g_ref, win_ref, wout_ref, o_ref, *, d_ff, ck):
    x = x_ref[...]
    h = (_rms(x) * g_ref[...]).astype(BF16)
    acc = jnp.zeros(x.shape, F32)
    for c in range(d_ff // ck):
        g = _dot(h, win_ref[:, c * ck:(c + 1) * ck])
        u = _dot(h, win_ref[:, d_ff + c * ck:d_ff + (c + 1) * ck])
        a = (g * _sigmoid(g) * u).astype(BF16)
        acc = acc + _dot(a, wout_ref[c * ck:(c + 1) * ck, :])
    o_ref[...] = x + 0.5 * acc


def _ffn(x, norm_g, w_in, w_out, tm):
    m, d = x.shape
    d_ff = w_out.shape[0]
    ck = 256
    assert m % tm == 0 and d_ff % ck == 0
    return pl.pallas_call(
        functools.partial(_ffn_kernel, d_ff=d_ff, ck=ck),
        grid=(m // tm,),
        in_specs=[pl.BlockSpec((tm, d), lambda i: (i, 0)),
                  _const_spec((1, d)),
                  _const_spec((d, 2 * d_ff)),
                  _const_spec((d_ff, d))],
        out_specs=pl.BlockSpec((tm, d), lambda i: (i, 0)),
        out_shape=jax.ShapeDtypeStruct((m, d), F32),
        compiler_params=_params(1),
        name="ffn",
    )(x, norm_g, w_in, w_out)


def _proj_kernel(x_ref, g_ref, wqk_ref, wv_ref, wu_ref, wbv_ref, wif_ref, bif_ref,
                 qn_ref, kn_ref, cos_ref, sin_ref, gmat_ref,
                 q_ref, k_ref, kb_ref, v_ref, vb_ref, u_ref, bv_ref, gc_ref, gr_ref):
    x = x_ref[...]
    tm = x.shape[0]
    h = (_rms(x) * g_ref[...]).astype(BF16)
    cos = cos_ref[...]
    sin = sin_ref[...]
    gmat = gmat_ref[...]
    lane = lax.broadcasted_iota(jnp.int32, (tm, LANES), 1)
    first_half = (lane % D_HEAD) < (D_HEAD // 2)

    def norm_rope(z2, nw):
        ms = _dot((z2 * z2).astype(BF16), gmat)
        y2 = z2 * lax.rsqrt(ms + EPS)
        outs = []
        for t in range(2):
            y = y2[:, t * LANES:(t + 1) * LANES] * nw
            rot = jnp.where(first_half, pltpu.roll(y, LANES - D_HEAD // 2, 1), pltpu.roll(y, D_HEAD // 2, 1))
            outs.append(y * cos + rot * sin)
        return outs

    qn = qn_ref[...]
    kn = kn_ref[...]
    zqk = _dot(h, wqk_ref[...])
    for pr in range(W_A // (2 * LANES)):
        c0 = pr * 2 * LANES
        for t, qq in enumerate(norm_rope(zqk[:, c0:c0 + 2 * LANES], qn)):
            q_ref[:, c0 + t * LANES:c0 + (t + 1) * LANES] = (qq * Q_SCALE).astype(BF16)
        for t, kk in enumerate(norm_rope(zqk[:, W_A + c0:W_A + c0 + 2 * LANES], kn)):
            k_ref[:, c0 + t * LANES:c0 + (t + 1) * LANES] = kk
            kb_ref[:, c0 + t * LANES:c0 + (t + 1) * LANES] = kk.astype(BF16)
    v = _dot(h, wv_ref[...])
    v_ref[...] = v
    vb_ref[...] = v.astype(BF16)
    u_ref[...] = _dot(h, wu_ref[...])
    bv_ref[...] = _dot(h, wbv_ref[...])
    z = _dot(h, wif_ref[...]) + bif_ref[...]
    gc = jnp.where(lane < H_B, z, _log_sigmoid(z))
    gc_ref[...] = gc
    gr_ref[...] = gc.T[:8, :]


def _proj(x, mix_norm, wp, cos_t, sin_t, tm, pos_blocks):
    m, d = x.shape
    assert m % tm == 0
    row = lambda w: pl.BlockSpec((tm, w), lambda i: (i, 0))
    tab = pl.BlockSpec((tm, LANES), lambda i: (i % pos_blocks, 0))
    outs = [
        jax.ShapeDtypeStruct((m, W_A), BF16),
        jax.ShapeDtypeStruct((m, W_A), F32),
        jax.ShapeDtypeStruct((m, W_A), BF16),
        jax.ShapeDtypeStruct((m, W_A), F32),
        jax.ShapeDtypeStruct((m, W_A), BF16),
        jax.ShapeDtypeStruct((m, W_U), F32),
        jax.ShapeDtypeStruct((m, W_BV), F32),
        jax.ShapeDtypeStruct((m, LANES), F32),
        jax.ShapeDtypeStruct((8, m), F32),
    ]
    return pl.pallas_call(
        _proj_kernel,
        grid=(m // tm,),
        in_specs=[row(d), _const_spec((1, d)),
                  _const_spec((d, 2 * W_A)), _const_spec((d, W_A)), _const_spec((d, W_U)),
                  _const_spec((d, W_BV)), _const_spec((d, LANES)), _const_spec((1, LANES)),
                  _const_spec((1, LANES)), _const_spec((1, LANES)), tab, tab,
                  _const_spec((2 * LANES, 2 * LANES))],
        out_specs=[row(W_A), row(W_A), row(W_A), row(W_A), row(W_A), row(W_U), row(W_BV),
                   row(LANES), pl.BlockSpec((8, tm), lambda i: (0, i))],
        out_shape=outs,
        compiler_params=_params(1),
        name="proj",
    )(x, mix_norm, wp["wqk"], wp["wv"], wp["wu"], wp["wbv"], wp["wif"], wp["bif"],
      wp["qn"], wp["kn"], cos_t, sin_t, wp["gmat"])


def _lambda(lam_ref):
    a = jnp.sum(lam_ref[0:1, :] * lam_ref[1:2, :], axis=-1, keepdims=True)
    b = jnp.sum(lam_ref[2:3, :] * lam_ref[3:4, :], axis=-1, keepdims=True)
    return jnp.exp(a) - jnp.exp(b) + LAM_INIT


def _attn_kernel(q_ref, k_ref, v_ref, km_ref, vm_ref, lam_ref, sn_ref, o_ref,
                 qq_s, s0_s, s1_s, m_s, acc_s, *, tq):
    i = pl.program_id(2)
    q = q_ref[0]
    lane = lax.broadcasted_iota(jnp.int32, (tq, LANES), 1)
    zero = jnp.zeros_like(q)
    qq_s[0:tq, :] = jnp.where(lane < D_HEAD, q, zero)
    qq_s[tq:2 * tq, :] = jnp.where(lane >= D_HEAD, q, zero)

    def with_ones(v):
        return jnp.concatenate([v, jnp.ones_like(v)], axis=1)

    def scores(j, s_ref):
        off = pl.multiple_of(j * tq, tq)
        s_ref[...] = _nt(qq_s[...], k_ref[0, pl.ds(off, tq), :])

    def consume(j, s_ref, masked):
        off = pl.multiple_of(j * tq, tq)
        s = s_ref[...]
        if masked:
            row = lax.broadcasted_iota(jnp.int32, s.shape, 0)
            colk = lax.broadcasted_iota(jnp.int32, s.shape, 1)
            qidx = jnp.where(row >= tq, row - tq, row)
            s = jnp.where(colk <= qidx, s, NEG_INF)
        m = m_s[...]
        m_new = jnp.maximum(m, jnp.max(s, axis=1, keepdims=True))
        p = jnp.exp2(s - jnp.concatenate([m_new] * (tq // LANES), axis=1)).astype(BF16)
        alpha = jnp.exp2(m - m_new)
        acc_s[...] = (jnp.concatenate([alpha, alpha], axis=1) * acc_s[...]
                      + _dot(p, with_ones(v_ref[0, pl.ds(off, tq), :])))
        m_s[...] = m_new

    s = _nt(qq_s[...], km_ref[...])
    col = lax.broadcasted_iota(jnp.int32, s.shape, 1)
    s = jnp.where(col < N_META, s, NEG_INF)
    m = jnp.max(s, axis=1, keepdims=True)
    m_s[...] = jnp.broadcast_to(m, (2 * tq, LANES))
    acc_s[...] = _dot(jnp.exp2(s - m).astype(BF16), with_ones(vm_ref[...]))

    scores(0, s0_s)

    def pair(t, _):
        scores(2 * t + 1, s1_s)
        consume(2 * t, s0_s, False)
        scores(2 * t + 2, s0_s)
        consume(2 * t + 1, s1_s, False)
        return 0

    lax.fori_loop(0, i // 2, pair, 0)

    @pl.when(i % 2 == 1)
    def _():
        scores(i, s1_s)
        consume(i - 1, s0_s, False)
        consume(i, s1_s, True)

    @pl.when(i % 2 == 0)
    def _():
        consume(i, s0_s, True)

    acc = acc_s[...]
    o = acc[:, :LANES] / acc[:, LANES:]
    oa = o[:tq] - _lambda(lam_ref) * o[tq:]
    o_ref[0] = (_rms(oa) * sn_ref[...] * (1.0 - LAM_INIT)).astype(BF16)


def _prompt_attn(q, kb, vb, kmeta, vmeta, lam, sub_norm, tq):
    b, s, _ = q.shape
    assert s % tq == 0
    return pl.pallas_call(
        functools.partial(_attn_kernel, tq=tq),
        grid=(b, H_A, s // tq),
        in_specs=[pl.BlockSpec((1, tq, LANES), lambda b, h, i: (b, i, h)),
                  pl.BlockSpec((1, s, LANES), lambda b, h, i: (b, 0, h)),
                  pl.BlockSpec((1, s, LANES), lambda b, h, i: (b, 0, h)),
                  pl.BlockSpec((LANES, LANES), lambda b, h, i: (0, h)),
                  pl.BlockSpec((LANES, LANES), lambda b, h, i: (0, h)),
                  pl.BlockSpec((4, D_HEAD), lambda b, h, i: (0, 0)),
                  pl.BlockSpec((1, LANES), lambda b, h, i: (0, 0))],
        out_specs=pl.BlockSpec((1, tq, LANES), lambda b, h, i: (b, i, h)),
        out_shape=jax.ShapeDtypeStruct((b, s, W_A), BF16),
        scratch_shapes=[pltpu.VMEM((2 * tq, LANES), BF16),
                        pltpu.VMEM((2 * tq, tq), F32), pltpu.VMEM((2 * tq, tq), F32),
                        pltpu.VMEM((2 * tq, LANES), F32), pltpu.VMEM((2 * tq, 2 * LANES), F32)],
        compiler_params=_params(3),
        name="prompt_attn",
    )(q, kb, vb, kmeta, vmeta, lam, sub_norm)


def _mlstm_kernel(u_ref, bv_ref, gc_ref, gr_ref, cw_ref, cb_ref, c0_ref, n0_ref, m0_ref, t0_ref,
                  hb_ref, c_ref, n_ref, m_ref, t_ref, ue_ref, *, el, nb):
    c = pl.program_id(0)

    @pl.when(c == 0)
    def _():
        c_ref[...] = c0_ref[...]
        n_ref[...] = n0_ref[...]
        m_ref[...] = m0_ref[...]
        t_ref[...] = t0_ref[...]

    ri = lax.broadcasted_iota(jnp.int32, (el, el), 0)
    ci = lax.broadcasted_iota(jnp.int32, (el, el), 1)
    causal = ci <= ri
    tri = jnp.where(causal, 1.0, 0.0).astype(BF16)
    tri_t = jnp.where(ri <= ci, 1.0, 0.0).astype(BF16)
    lane = lax.broadcasted_iota(jnp.int32, (el, LANES), 1)

    qk, gc, gr, fh_cols, fh_rows = [], [], [], [], []
    for bi in range(nb):
        ue_ref[bi, 0:8, :] = t_ref[bi]
        ue_ref[bi, 8:8 + el, :] = u_ref[bi]
        conv = cb_ref[...] + ue_ref[bi, 5:5 + el, :] * cw_ref[0:1, :]
        for j in range(1, CONV_W):
            conv = conv + ue_ref[bi, 5 + j:5 + j + el, :] * cw_ref[j:j + 1, :]
        t_ref[bi] = ue_ref[bi, el:el + 8, :]
        qk.append(conv * _sigmoid(conv))
        gc.append(gc_ref[bi])
        gr.append(gr_ref[bi])
        g1, g2, g3 = _split3(gc[bi])
        fh_cols.append(_dot(tri, g1) + _dot(tri, g2) + _dot(tri, g3))
        r1, r2, r3 = _split3(gr[bi])
        fh_rows.append(_dot(r1, tri_t) + _dot(r2, tri_t) + _dot(r3, tri_t))

    groups = [(bi, h) for bi in range(nb) for h in range(H_B)]
    mine = [(lane >= DQK) if (h % 2) else (lane < DQK) for h in range(H_B)]
    qp = lambda bi, h: qk[bi][:, (h // 2) * LANES:(h // 2 + 1) * LANES]
    kp = lambda bi, h: qk[bi][:, H_B * DQK + (h // 2) * LANES:H_B * DQK + (h // 2 + 1) * LANES] * (DQK ** -0.5)
    qm = jnp.stack([jnp.where(mine[h], qp(bi, h), 0.0) for bi, h in groups])
    km = jnp.stack([jnp.where(mine[h], kp(bi, h), 0.0) for bi, h in groups])
    kpb = jnp.stack([kp(bi, h) for bi, h in groups]).astype(BF16)
    v3 = jnp.stack([bv_ref[bi, :, h * DV:(h + 1) * DV] for bi, h in groups])
    rep = lambda col: jnp.broadcast_to(col, (el, LANES))
    fhc = jnp.stack([rep(fh_cols[bi][:, H_B + h:H_B + h + 1]) for bi, h in groups])
    igc = jnp.stack([rep(gc[bi][:, h:h + 1]) for bi, h in groups])
    fhr = jnp.stack([fh_rows[bi][H_B + h:H_B + h + 1, :] for bi, h in groups])
    igr = jnp.stack([gr[bi][h:h + 1, :] for bi, h in groups])
    m0 = jnp.stack([m_ref[bi, h:h + 1, :] for bi, h in groups])
    n0 = jnp.stack([n_ref[bi, h:h + 1, :] for bi, h in groups])
    c0 = c_ref[...].reshape(nb * H_B, DV, LANES)

    dmat = jnp.where(causal[None], fhc[:, :, :el] - fhr + igr, NEG_INF)
    b_in = fhc + m0
    m = jnp.maximum(b_in, jnp.max(dmat, axis=2, keepdims=True))
    wts = jnp.exp(dmat - m[:, :, :el])
    inter = jnp.exp(b_in - m)
    qmb = qm.astype(BF16)
    s = (_bdot(qmb, kpb, 2, 2) * wts).astype(BF16)
    sv = _bdot(s, jnp.concatenate([v3, jnp.ones_like(v3)], axis=2).astype(BF16), 2, 1)
    qc = _bdot(qmb, jnp.concatenate([c0, jnp.broadcast_to(n0, c0.shape)], axis=1).astype(BF16), 2, 2)
    num = sv[:, :, :DV] + inter * qc[:, :, :DV]
    den = sv[:, :, DV:] + inter * qc[:, :, DV:]
    hn = _rms(num / jnp.maximum(jnp.abs(den), jnp.exp(-m)))

    m_new = m[:, el - 1:el, :]
    fl = fhc[:, el - 1:el, :]
    wl = jnp.exp(fl - fhc + igc - m_new)
    decay = jnp.exp(fl + m0 - m_new)
    c_new = decay * c0 + _bdot((v3 * wl).astype(BF16), km.astype(BF16), 1, 1)
    n_new = decay * n0 + jnp.sum(wl * km, axis=1, keepdims=True)
    c_ref[...] = c_new.reshape(nb, H_B, DV, LANES)
    for g, (bi, h) in enumerate(groups):
        hb_ref[bi, :, h * DV:(h + 1) * DV] = hn[g]
        n_ref[bi, h:h + 1, :] = n_new[g]
        m_ref[bi, h:h + 1, :] = m_new[g]


def _mlstm(u, bv, gc, gr, conv_w, conv_b, c0, n0, m0, t0, el):
    b, s, _ = u.shape
    nc = s // el
    assert s % el == 0 and gr.shape == (b, 8, s)
    st = lambda shape: pl.BlockSpec((b,) + shape, lambda c: (0,) * (len(shape) + 1))
    seq = lambda w: pl.BlockSpec((b, el, w), lambda c: (0, c, 0))
    outs = [jax.ShapeDtypeStruct((b, s, W_BV), F32),
            jax.ShapeDtypeStruct((b, H_B, DV, LANES), F32),
            jax.ShapeDtypeStruct((b, H_B, LANES), F32),
            jax.ShapeDtypeStruct((b, H_B, LANES), F32),
            jax.ShapeDtypeStruct((b, 8, W_U), F32)]
    return pl.pallas_call(
        functools.partial(_mlstm_kernel, el=el, nb=b),
        grid=(nc,),
        in_specs=[seq(W_U), seq(W_BV), seq(LANES),
                  pl.BlockSpec((b, 8, el), lambda c: (0, 0, c)),
                  pl.BlockSpec((CONV_W, W_U), lambda c: (0, 0)),
                  pl.BlockSpec((1, W_U), lambda c: (0, 0)),
                  st((H_B, DV, LANES)), st((H_B, LANES)), st((H_B, LANES)), st((8, W_U))],
        out_specs=[seq(W_BV), st((H_B, DV, LANES)), st((H_B, LANES)), st((H_B, LANES)), st((8, W_U))],
        out_shape=outs,
        scratch_shapes=[pltpu.VMEM((b, el + 8, W_U), F32)],
        compiler_params=_params(1),
        name="mlstm_scan",
    )(u, bv, gc, gr, conv_w, conv_b, c0, n0, m0, t0)


def _mixer_kernel(x_ref, ya_ref, hb_ref, g_ref, wg_ref, bg_ref, wbo_ref, mh_ref, pa_ref, pb_ref, wo_ref, o_ref):
    x = x_ref[...]
    d = x.shape[1]
    h = (_rms(x) * g_ref[...]).astype(BF16)
    bo = _dot(h, wbo_ref[...])
    yb = (hb_ref[...] * mh_ref[...] * _sigmoid(bo)).astype(BF16)
    pa = _dot(ya_ref[...], pa_ref[...])
    pb = _dot(yb, pb_ref[...])
    ga = _sigmoid(_dot(h, wg_ref[:, :d]) + bg_ref[:, :d])
    gb = _sigmoid(_dot(h, wg_ref[:, d:]) + bg_ref[:, d:])
    merged = (ga * pa + gb * pb).astype(BF16)
    o_ref[...] = x + _dot(merged, wo_ref[...])


def _mixer(x, ya, hbn, mix_norm, wp, tm):
    m, d = x.shape
    assert m % tm == 0
    row = lambda w: pl.BlockSpec((tm, w), lambda i: (i, 0))
    return pl.pallas_call(
        _mixer_kernel,
        grid=(m // tm,),
        in_specs=[row(d), row(W_A), row(W_BV), _const_spec((1, d)),
                  _const_spec((d, 2 * d)), _const_spec((1, 2 * d)), _const_spec((d, W_BV)),
                  _const_spec((1, W_BV)), _const_spec((W_A, d)), _const_spec((W_BV, d)),
                  _const_spec((d, d))],
        out_specs=row(d),
        out_shape=jax.ShapeDtypeStruct((m, d), F32),
        compiler_params=_params(1),
        name="mixer_out",
    )(x, ya, hbn, mix_norm, wp["wg"], wp["bg"], wp["wbo"], wp["mh"], wp["pa"], wp["pb"], wp["wo"])


def _decode_kernel(pt_ref, qrow_ref, q_ref, kn_ref, vn_ref, lam_ref, sn_ref, *refs, pps):
    k_refs = refs[:pps]
    v_refs = refs[pps:2 * pps]
    o_ref, m_s, l_s, acc_s = refs[2 * pps:]
    g = pl.program_id(1)
    ng = pl.num_programs(1)
    nsub = 2 * H_A
    page = k_refs[0].shape[1]

    @pl.when(g == 0)
    def _():
        m_s[...] = jnp.full(m_s.shape, NEG_INF, F32)
        l_s[...] = jnp.zeros(l_s.shape, F32)
        acc_s[...] = jnp.zeros(acc_s.shape, F32)

    row = lax.broadcasted_iota(jnp.int32, (nsub, W_A), 0)
    lane = lax.broadcasted_iota(jnp.int32, (nsub, W_A), 1)
    qbd = jnp.where(lane // D_HEAD == row, jnp.broadcast_to(qrow_ref[0].astype(F32), (nsub, W_A)), 0.0).astype(BF16)

    s = jnp.concatenate([_dot(qbd, k_refs[i][...].astype(BF16)) for i in range(pps)], axis=1)
    m = m_s[...]
    m_new = jnp.maximum(m, jnp.max(s, axis=1, keepdims=True))
    alpha = jnp.exp2(m - m_new)
    p = jnp.exp2(s - m_new)
    l_s[...] = alpha * l_s[...] + jnp.sum(p, axis=1, keepdims=True)
    m_s[...] = m_new
    pb = p.astype(BF16)
    for h in range(H_A):
        pv = _dot(pb[:, 0:page], v_refs[0][pl.ds(h, page, stride=H_A), :].astype(BF16))
        for i in range(1, pps):
            pv = pv + _dot(pb[:, i * page:(i + 1) * page], v_refs[i][pl.ds(h, page, stride=H_A), :].astype(BF16))
        acc_s[h] = alpha * acc_s[h] + pv

    @pl.when(g == ng - 1)
    def _():
        q8 = q_ref[0].astype(F32)
        s_self = jnp.sum(q8 * kn_ref[0].astype(BF16).astype(F32), axis=1, keepdims=True)
        m = m_s[...]
        m_new = jnp.maximum(m, s_self)
        alpha = jnp.exp2(m - m_new)
        p_self = jnp.exp2(s_self - m_new)
        l = alpha * l_s[...] + p_self
        lam = _lambda(lam_ref)
        vn = vn_ref[0].astype(BF16).astype(F32)
        outs = []
        for h in range(H_A):
            o = (alpha * acc_s[h] + p_self * vn[h:h + 1, :]) / l
            oa = o[2 * h:2 * h + 1, :] - lam * o[2 * h + 1:2 * h + 2, :]
            outs.append(_rms(oa) * sn_ref[...] * (1.0 - LAM_INIT))
        o_ref[0] = jnp.concatenate(outs, axis=1).astype(BF16)


def _decode_attn(page_table, q, k_new, v_new, cache_k, cache_v, lam, sub_norm, pps):
    b, n_pages = page_table.shape
    n_pool, page = cache_k.shape[1], cache_k.shape[2]
    nsub = 2 * H_A
    assert n_pages % pps == 0 and cache_k.shape[0] == 1
    pt = page_table.reshape(-1)
    ck = jnp.transpose(cache_k[0], (0, 2, 3, 1)).reshape(n_pool, nsub * D_HEAD, page)
    cv = cache_v[0].reshape(n_pool, page * H_A, 2 * D_HEAD)

    kspec = lambda i: pl.BlockSpec((None, nsub * D_HEAD, page),
                                   lambda b, g, pt: (pt[b * n_pages + g * pps + i], 0, 0))
    vspec = lambda i: pl.BlockSpec((None, page * H_A, 2 * D_HEAD),
                                   lambda b, g, pt: (pt[b * n_pages + g * pps + i], 0, 0))
    tok = lambda heads, width: pl.BlockSpec((1, heads, width), lambda b, g, pt: (b, 0, 0))
    grid_spec = pltpu.PrefetchScalarGridSpec(
        num_scalar_prefetch=1,
        grid=(b, n_pages // pps),
        in_specs=[tok(1, W_A), tok(nsub, D_HEAD), tok(nsub, D_HEAD), tok(H_A, 2 * D_HEAD),
                  pl.BlockSpec((4, D_HEAD), lambda b, g, pt: (0, 0)),
                  pl.BlockSpec((1, LANES), lambda b, g, pt: (0, 0))]
                 + [kspec(i) for i in range(pps)] + [vspec(i) for i in range(pps)],
        out_specs=pl.BlockSpec((1, 1, W_A), lambda b, g, pt: (b, 0, 0)),
        scratch_shapes=[pltpu.VMEM((nsub, 1), F32), pltpu.VMEM((nsub, 1), F32),
                        pltpu.VMEM((H_A, nsub, 2 * D_HEAD), F32)],
    )
    return pl.pallas_call(
        functools.partial(_decode_kernel, pps=pps),
        grid_spec=grid_spec,
        out_shape=jax.ShapeDtypeStruct((b, 1, W_A), BF16),
        compiler_params=_params(2),
        name="decode_attn",
    )(pt, q.reshape(b, 1, W_A), q, k_new, v_new, lam, sub_norm, *([ck] * pps), *([cv] * pps))


def _mlstm_step_kernel(u_ref, cs_ref, bv_ref, gc_ref, cw_ref, cb_ref, c0_ref, n0_ref, m0_ref,
                       hb_ref, c_ref, n_ref, m_ref, cso_ref, *, nb):
    u = u_ref[...]
    cs = cs_ref[...]
    conv = cb_ref[...] + u * cw_ref[CONV_W - 1:CONV_W, :]
    for j in range(CONV_W - 1):
        conv = conv + cs[:, j * W_U:(j + 1) * W_U] * cw_ref[j:j + 1, :]
    cso_ref[...] = jnp.concatenate([cs[:, W_U:], u], axis=1)
    qk = conv * _sigmoid(conv)
    gc = gc_ref[...]
    bv = bv_ref[...]
    m0_all = m0_ref[...]
    ident = jnp.where(lax.broadcasted_iota(jnp.int32, (DV, DV), 0) == lax.broadcasted_iota(jnp.int32, (DV, DV), 1),
                      1.0, 0.0).astype(BF16)
    m_cols = []
    for h in range(H_B):
        qh = qk[:, h * DQK:(h + 1) * DQK]
        kh = qk[:, H_B * DQK + h * DQK:H_B * DQK + (h + 1) * DQK] * (DQK ** -0.5)
        vh = bv[:, h * DV:(h + 1) * DV]
        ig = gc[:, h:h + 1]
        logf = gc[:, H_B + h:H_B + h + 1]
        m0 = m0_all[:, h:h + 1]
        n0 = n0_ref[:, h, :]
        b_in = logf + m0
        m = jnp.maximum(b_in, ig)
        wts = jnp.exp(ig - m)
        inter = jnp.exp(b_in - m)
        s = jnp.sum(qh * kh, axis=1, keepdims=True) * wts
        qhb = qh.astype(BF16)
        cq_rows = []
        for b in range(nb):
            cq_rows.append(_nt(qhb, c0_ref[b, h].astype(BF16))[b:b + 1, :])
        cq = jnp.concatenate(cq_rows, axis=0)
        num = s * vh + inter * cq
        den = s + inter * jnp.sum(n0 * qh, axis=1, keepdims=True)
        hout = num / jnp.maximum(jnp.abs(den), jnp.exp(-m))
        hb_ref[:, h * DV:(h + 1) * DV] = _rms(hout)
        vcols = _nt(ident, (vh * wts).astype(BF16))
        for b in range(nb):
            c_ref[b, h] = inter[b:b + 1, :] * c0_ref[b, h] + vcols[:, b:b + 1] * kh[b:b + 1, :]
        n_ref[:, h, :] = inter * n0 + wts * kh
        m_cols.append(m)
    m_ref[...] = jnp.concatenate(m_cols, axis=1)


def _mlstm_step(u, conv_state, bv, gc, conv_w, conv_b, c0, n0, m0, nb):
    b = u.shape[0]
    assert b % nb == 0
    row = lambda w: pl.BlockSpec((nb, w), lambda i: (i, 0))
    outs = [jax.ShapeDtypeStruct((b, W_BV), F32),
            jax.ShapeDtypeStruct((b, H_B, DV, DQK), F32),
            jax.ShapeDtypeStruct((b, H_B, DQK), F32),
            jax.ShapeDtypeStruct((b, H_B), F32),
            jax.ShapeDtypeStruct((b, (CONV_W - 1) * W_U), F32)]
    cspec = pl.BlockSpec((nb, H_B, DV, DQK), lambda i: (i, 0, 0, 0))
    nspec = pl.BlockSpec((nb, H_B, DQK), lambda i: (i, 0, 0))
    return pl.pallas_call(
        functools.partial(_mlstm_step_kernel, nb=nb),
        grid=(b // nb,),
        in_specs=[row(W_U), row((CONV_W - 1) * W_U), row(W_BV), row(LANES),
                  pl.BlockSpec((CONV_W, W_U), lambda i: (0, 0)),
                  pl.BlockSpec((1, W_U), lambda i: (0, 0)),
                  cspec, nspec, row(H_B)],
        out_specs=[row(W_BV), cspec, nspec, row(H_B), row((CONV_W - 1) * W_U)],
        out_shape=outs,
        compiler_params=_params(1),
        name="mlstm_step",
    )(u, conv_state, bv, gc, conv_w, conv_b, c0, n0, m0)


def _rope_tables(pos):
    half = D_HEAD // 2
    inv = 1.0 / (ROPE_THETA ** (jnp.arange(half, dtype=F32) / half))
    ang = pos.astype(F32)[:, None] * inv[None, :]
    cos = jnp.cos(ang)
    sin = jnp.sin(ang)
    return jnp.tile(cos, (1, 4)), jnp.tile(jnp.concatenate([-sin, sin], axis=1), (1, 2))


def kernel(x_prompt, x_sample, cache_k, cache_v, page_table, state_C, state_n, state_m, state_conv, meta_tokens, ffn1_norm, ffn1_w_in, ffn1_w_out, mix_norm, w_in, b_if, b_gate, q_norm, k_norm, lam_q1, lam_k1, lam_q2, lam_k2, sub_norm, conv_w, conv_b, mh_norm, p_a, p_b, w_o, ffn2_norm, ffn2_w_in, ffn2_w_out):
    bp, s, d = x_prompt.shape
    bs = x_sample.shape[0]
    assert x_sample.shape[1] == 1 and w_in.shape[0] == 1
    n_pages = page_table.shape[1]
    page = cache_k.shape[2]

    w = w_in[0]
    o = 0
    cols = {}
    for name, width in (("q", W_A), ("k", W_A), ("v", W_A), ("u", W_U), ("bv", W_BV), ("bo", W_BV),
                        ("if", 2 * H_B), ("g", 2 * d)):
        cols[name] = w[:, o:o + width]
        o += width
    gidx = jnp.arange(2 * LANES) // D_HEAD
    wp = dict(
        wqk=jnp.concatenate([cols["q"], cols["k"]], axis=1).astype(BF16),
        wv=cols["v"].astype(BF16), wu=cols["u"].astype(BF16), wbv=cols["bv"].astype(BF16),
        wif=jnp.pad(cols["if"], ((0, 0), (0, LANES - 2 * H_B))).astype(BF16),
        bif=jnp.pad(b_if[0], (0, LANES - 2 * H_B)).reshape(1, LANES),
        qn=jnp.tile(q_norm[0], LANES // D_HEAD).reshape(1, LANES),
        kn=jnp.tile(k_norm[0], LANES // D_HEAD).reshape(1, LANES),
        gmat=jnp.where(gidx[:, None] == gidx[None, :], 1.0 / D_HEAD, 0.0).astype(BF16),
        wg=cols["g"].astype(BF16), bg=b_gate[0].reshape(1, 2 * d), wbo=cols["bo"].astype(BF16),
        mh=mh_norm[0].reshape(1, W_BV), pa=p_a[0].astype(BF16), pb=p_b[0].astype(BF16),
        wo=w_o[0].astype(BF16),
    )
    f1 = (ffn1_norm[0].reshape(1, d), ffn1_w_in[0].astype(BF16), ffn1_w_out[0].astype(BF16))
    f2 = (ffn2_norm[0].reshape(1, d), ffn2_w_in[0].astype(BF16), ffn2_w_out[0].astype(BF16))
    mixn = mix_norm[0].reshape(1, d)
    lam = jnp.stack([lam_q1[0], lam_k1[0], lam_q2[0], lam_k2[0]])
    subn = sub_norm[0].reshape(1, 2 * D_HEAD)
    cw, cb = conv_w[0], conv_b[0].reshape(1, W_U)

    tm = 512 if (bp * s) % 512 == 0 else 256
    xr = x_prompt.reshape(bp * s, d)
    n_small = -(-(bs + N_META) // 256) * 256
    xs = jnp.concatenate([x_sample.reshape(bs, d), meta_tokens.astype(F32),
                          jnp.zeros((n_small - bs - N_META, d), F32)], axis=0)
    past = n_pages * page
    pos_small = jnp.concatenate([jnp.full((bs,), past, jnp.int32), jnp.arange(N_META, dtype=jnp.int32),
                                 jnp.zeros((n_small - bs - N_META,), jnp.int32)])
    cos_r, sin_r = _rope_tables(N_META + jnp.arange(s))
    cos_s, sin_s = _rope_tables(pos_small)

    x1r = _ffn(xr, *f1, tm)
    x1s = _ffn(xs, *f1, n_small)
    q_r, k_r, kb_r, v_r, vb_r, u_r, bv_r, gc_r, gr_r = _proj(x1r, mixn, wp, cos_r, sin_r, tm, s // tm)
    q_s, k_s, _, v_s, _, u_s, bv_s, gc_s, gr_s = _proj(x1s, mixn, wp, cos_s, sin_s, n_small, 1)

    msl = slice(bs, bs + N_META)
    k_meta, v_meta = k_s[msl], v_s[msl]
    pad_meta = lambda a: jnp.pad(a, ((0, LANES - N_META), (0, 0))).astype(BF16)
    _, c_m, n_m, m_m, t_m = _mlstm(
        u_s[msl][None], bv_s[msl][None], gc_s[msl][None], gr_s[:, msl][None], cw, cb,
        jnp.zeros((1, H_B, DV, LANES), F32), jnp.zeros((1, H_B, LANES), F32),
        jnp.full((1, H_B, LANES), NEG_INF, F32), jnp.zeros((1, 8, W_U), F32), N_META)

    tq = 512 if s % 512 == 0 else 256
    ya_r =_prompt_attn(q_r.reshape(bp, s, W_A), kb_r.reshape(bp, s, W_A), vb_r.reshape(bp, s, W_A),
                        pad_meta(k_meta), pad_meta(v_meta), lam, subn, tq)
    rep = lambda a: jnp.broadcast_to(a, (bp,) + a.shape[1:])
    hb_r, c_p, n_p, m_p, _ = _mlstm(
        u_r.reshape(bp, s, W_U), bv_r.reshape(bp, s, W_BV), gc_r.reshape(bp, s, LANES),
        jnp.swapaxes(gr_r.reshape(8, bp, s), 0, 1), cw, cb,
        rep(c_m), rep(n_m), rep(m_m), rep(t_m), CHUNK)

    ssl = slice(0, bs)
    pps = 16 if n_pages % 16 == 0 else n_pages
    ya_s = _decode_attn(page_table, q_s[ssl].reshape(bs, 2 * H_A, D_HEAD), k_s[ssl].reshape(bs, 2 * H_A, D_HEAD),
                        v_s[ssl].reshape(bs, H_A, 2 * D_HEAD), cache_k, cache_v, lam, subn, pps)
    hb_s, c_s, n_s, m_s, conv_s = _mlstm_step(
        u_s[ssl], state_conv[0].reshape(bs, (CONV_W - 1) * W_U), bv_s[ssl], gc_s[ssl], cw, cb,
        state_C[0], state_n[0], state_m[0], 8)

    y_r = _ffn(_mixer(x1r, ya_r.reshape(bp * s, W_A), hb_r.reshape(bp * s, W_BV), mixn, wp, tm), *f2, tm)
    y_s = _ffn(_mixer(x1s[ssl], ya_s.reshape(bs, W_A), hb_s, mixn, wp, bs), *f2, bs)

    def with_meta(meta, real):
        return jnp.concatenate([jnp.broadcast_to(meta[None], (bp, N_META, W_A)), real.reshape(bp, s, W_A)], axis=1)

    def unpad_heads(a):
        return jnp.stack([a[:, h, ..., (h % 2) * DQK:(h % 2 + 1) * DQK] for h in range(H_B)], axis=1)

    k_prompt = with_meta(k_meta, k_r).reshape(1, bp, N_META + s, 2 * H_A, D_HEAD)
    v_prompt = with_meta(v_meta, v_r).reshape(1, bp, N_META + s, H_A, 2 * D_HEAD)
    return (y_r.reshape(bp, s, d), y_s.reshape(bs, 1, d), k_prompt, v_prompt,
            k_s[ssl].reshape(1, bs, 1, 2 * H_A, D_HEAD), v_s[ssl].reshape(1, bs, 1, H_A, 2 * D_HEAD),
            unpad_heads(c_p)[None], unpad_heads(n_p[:, :, None, :])[:, :, 0][None], m_p[:, :, 0][None],
            u_r.reshape(bp, s, W_U)[:, s - (CONV_W - 1):][None],
            c_s[None], n_s[None], m_s[None], conv_s.reshape(1, bs, CONV_W - 1, W_U))
```

```python
import functools

import jax
import jax.numpy as jnp
from jax import lax
from jax.experimental import pallas as pl
from jax.experimental.pallas import tpu as pltpu

F32 = jnp.float32
BF16 = jnp.bfloat16

N_META = 16
H_A = 4
D_HEAD = 64
H_B = 4
DQK = 64
DV = 128
CONV_W = 4
CHUNK = 128
ROPE_THETA = 10000.0
EPS = 1e-6
ATTN_SCALE = D_HEAD ** -0.5
Q_SCALE = ATTN_SCALE * 1.4426950408889634
LAM_INIT = 0.2
LANES = 128
W_A = 2 * H_A * D_HEAD
W_U = 2 * H_B * DQK
W_BV = H_B * DV
VMEM_LIMIT = 56 * 1024 * 1024
NEG_INF = float("-inf")


def _dot(a, b):
    return jnp.dot(a, b, preferred_element_type=F32)


def _nt(a, b):
    return lax.dot_general(a, b, (((1,), (1,)), ((), ())), preferred_element_type=F32)


def _bdot(a, b, ca, cb):
    return lax.dot_general(a, b, (((ca,), (cb,)), ((0,), (0,))), preferred_element_type=F32)


def _rms(x):
    return x * lax.rsqrt(jnp.mean(x * x, axis=-1, keepdims=True) + EPS)


def _sigmoid(x):
    return 1.0 / (1.0 + jnp.exp(-x))


def _log_sigmoid(x):
    return jnp.minimum(x, 0.0) - jnp.log1p(jnp.exp(-jnp.abs(x)))


def _split3(x):
    x1 = x.astype(BF16)
    r = x - x1.astype(F32)
    x2 = r.astype(BF16)
    x3 = (r - x2.astype(F32)).astype(BF16)
    return x1, x2, x3


def _params(n_axes):
    return pltpu.CompilerParams(dimension_semantics=("arbitrary",) * n_axes,
                                vmem_limit_bytes=VMEM_LIMIT)


def _const_spec(shape):
    nd = len(shape)
    return pl.BlockSpec(shape, lambda *_: (0,) * nd, pipeline_mode=pl.Buffered(1))


def _ffn_kernel(x_ref, g_ref, win_ref, wout_ref, o_ref, *, d_ff, ck):
    x = x_ref[...]
    h = (_rms(x) * g_ref[...]).astype(BF16)
    acc = jnp.zeros(x.shape, F32)
    for c in range(d_ff // ck):
        g = _dot(h, win_ref[:, c * ck:(c + 1) * ck])
        u = _dot(h, win_ref[:, d_ff + c * ck:d_ff + (c + 1) * ck])
        a = (g * _sigmoid(g) * u).astype(BF16)
        acc = acc + _dot(a, wout_ref[c * ck:(c + 1) * ck, :])
    o_ref[...] = x + 0.5 * acc


def _ffn(x, norm_g, w_in, w_out, tm):
    m, d = x.shape
    d_ff = w_out.shape[0]
    ck = 256
    assert m % tm == 0 and d_ff % ck == 0
    return pl.pallas_call(
        functools.partial(_ffn_kernel, d_ff=d_ff, ck=ck),
        grid=(m // tm,),
        in_specs=[pl.BlockSpec((tm, d), lambda i: (i, 0)),
                  _const_spec((1, d)),
                  _const_spec((d, 2 * d_ff)),
                  _const_spec((d_ff, d))],
        out_specs=pl.BlockSpec((tm, d), lambda i: (i, 0)),
        out_shape=jax.ShapeDtypeStruct((m, d), F32),
        compiler_params=_params(1),
        name="ffn",
    )(x, norm_g, w_in, w_out)


def _proj_kernel(x_ref, g_ref, wqk_ref, wv_ref, wu_ref, wbv_ref, wif_ref, bif_ref,
                 qn_ref, kn_ref, cos_ref, sin_ref, gmat_ref,
                 q_ref, kt_ref, kb_ref, v4_ref, vb_ref, u_ref, bv_ref, gc_ref, gr_ref):
    x = x_ref[...]
    tm = x.shape[0]
    h = (_rms(x) * g_ref[...]).astype(BF16)
    cos = cos_ref[...]
    sin = sin_ref[...]
    gmat = gmat_ref[...]
    lane = lax.broadcasted_iota(jnp.int32, (tm, LANES), 1)
    first_half = (lane % D_HEAD) < (D_HEAD // 2)

    def norm_rope(z2, nw):
        ms = _dot((z2 * z2).astype(BF16), gmat)
        y2 = z2 * lax.rsqrt(ms + EPS)
        outs = []
        for t in range(2):
            y = y2[:, t * LANES:(t + 1) * LANES] * nw
            rot = jnp.where(first_half, pltpu.roll(y, LANES - D_HEAD // 2, 1), pltpu.roll(y, D_HEAD // 2, 1))
            outs.append(y * cos + rot * sin)
        return outs

    qn = qn_ref[...]
    kn = kn_ref[...]
    zqk = _dot(h, wqk_ref[...])
    for pr in range(W_A // (2 * LANES)):
        c0 = pr * 2 * LANES
        for t, qq in enumerate(norm_rope(zqk[:, c0:c0 + 2 * LANES], qn)):
            q_ref[:, c0 + t * LANES:c0 + (t + 1) * LANES] = (qq * Q_SCALE).astype(BF16)
        for t, kk in enumerate(norm_rope(zqk[:, W_A + c0:W_A + c0 + 2 * LANES], kn)):
            kt_ref[c0 + t * LANES:c0 + (t + 1) * LANES, :] = kk.T
            kb_ref[:, c0 + t * LANES:c0 + (t + 1) * LANES] = kk.astype(BF16)
    v = _dot(h, wv_ref[...])
    for hd in range(H_A):
        v4_ref[pl.ds(hd, tm, stride=H_A), :] = v[:, hd * 2 * D_HEAD:(hd + 1) * 2 * D_HEAD]
    vb_ref[...] = v.astype(BF16)
    u_ref[...] = _dot(h, wu_ref[...])
    bv_ref[...] = _dot(h, wbv_ref[...])
    z = _dot(h, wif_ref[...]) + bif_ref[...]
    gc = jnp.where(lane < H_B, z, _log_sigmoid(z))
    gc_ref[...] = gc
    gr_ref[...] = gc.T[:8, :]


def _proj(x, mix_norm, wp, cos_t, sin_t, tm, pos_blocks):
    m, d = x.shape
    seq = pos_blocks * tm
    assert m % seq == 0
    row = lambda w: pl.BlockSpec((tm, w), lambda i: (i, 0))
    tab = pl.BlockSpec((tm, LANES), lambda i: (i % pos_blocks, 0))
    outs = [
        jax.ShapeDtypeStruct((m, W_A), BF16),
        jax.ShapeDtypeStruct((m // seq, W_A, seq), F32),
        jax.ShapeDtypeStruct((m, W_A), BF16),
        jax.ShapeDtypeStruct((m * H_A, 2 * D_HEAD), F32),
        jax.ShapeDtypeStruct((m, W_A), BF16),
        jax.ShapeDtypeStruct((m, W_U), F32),
        jax.ShapeDtypeStruct((m, W_BV), F32),
        jax.ShapeDtypeStruct((m, LANES), F32),
        jax.ShapeDtypeStruct((8, m), F32),
    ]
    return pl.pallas_call(
        _proj_kernel,
        grid=(m // tm,),
        in_specs=[row(d), _const_spec((1, d)),
                  _const_spec((d, 2 * W_A)), _const_spec((d, W_A)), _const_spec((d, W_U)),
                  _const_spec((d, W_BV)), _const_spec((d, LANES)), _const_spec((1, LANES)),
                  _const_spec((1, LANES)), _const_spec((1, LANES)), tab, tab,
                  _const_spec((2 * LANES, 2 * LANES))],
        out_specs=[row(W_A),
                   pl.BlockSpec((None, W_A, tm), lambda i: (i // pos_blocks, 0, i % pos_blocks)),
                   row(W_A),
                   pl.BlockSpec((tm * H_A, 2 * D_HEAD), lambda i: (i, 0)),
                   row(W_A), row(W_U), row(W_BV),
                   row(LANES), pl.BlockSpec((8, tm), lambda i: (0, i))],
        out_shape=outs,
        compiler_params=_params(1),
        name="proj",
    )(x, mix_norm, wp["wqk"], wp["wv"], wp["wu"], wp["wbv"], wp["wif"], wp["bif"],
      wp["qn"], wp["kn"], cos_t, sin_t, wp["gmat"])


def _lambda(lam_ref):
    a = jnp.sum(lam_ref[0:1, :] * lam_ref[1:2, :], axis=-1, keepdims=True)
    b = jnp.sum(lam_ref[2:3, :] * lam_ref[3:4, :], axis=-1, keepdims=True)
    return jnp.exp(a) - jnp.exp(b) + LAM_INIT


def _attn_kernel(q_ref, k_ref, v_ref, km_ref, vm_ref, lam_ref, sn_ref, o_ref,
                 qq_s, s0_s, s1_s, m_s, acc_s, *, tq):
    i = pl.program_id(2)
    q = q_ref[0]
    lane = lax.broadcasted_iota(jnp.int32, (tq, LANES), 1)
    zero = jnp.zeros_like(q)
    qq_s[0:tq, :] = jnp.where(lane < D_HEAD, q, zero)
    qq_s[tq:2 * tq, :] = jnp.where(lane >= D_HEAD, q, zero)

    def with_ones(v):
        return jnp.concatenate([v, jnp.ones_like(v)], axis=1)

    def scores(j, s_ref):
        off = pl.multiple_of(j * tq, tq)
        s_ref[...] = _nt(qq_s[...], k_ref[0, pl.ds(off, tq), :])

    def consume(j, s_ref, masked):
        off = pl.multiple_of(j * tq, tq)
        s = s_ref[...]
        if masked:
            row = lax.broadcasted_iota(jnp.int32, s.shape, 0)
            colk = lax.broadcasted_iota(jnp.int32, s.shape, 1)
            qidx = jnp.where(row >= tq, row - tq, row)
            s = jnp.where(colk <= qidx, s, NEG_INF)
        m = m_s[...]
        m_new = jnp.maximum(m, jnp.max(s, axis=1, keepdims=True))
        p = jnp.exp2(s - jnp.concatenate([m_new] * (tq // LANES), axis=1)).astype(BF16)
        alpha = jnp.exp2(m - m_new)
        acc_s[...] = (jnp.concatenate([alpha, alpha], axis=1) * acc_s[...]
                      + _dot(p, with_ones(v_ref[0, pl.ds(off, tq), :])))
        m_s[...] = m_new

    s = _nt(qq_s[...], km_ref[...])
    col = lax.broadcasted_iota(jnp.int32, s.shape, 1)
    s = jnp.where(col < N_META, s, NEG_INF)
    m = jnp.max(s, axis=1, keepdims=True)
    m_s[...] = jnp.broadcast_to(m, (2 * tq, LANES))
    acc_s[...] = _dot(jnp.exp2(s - m).astype(BF16), with_ones(vm_ref[...]))

    scores(0, s0_s)

    def pair(t, _):
        scores(2 * t + 1, s1_s)
        consume(2 * t, s0_s, False)
        scores(2 * t + 2, s0_s)
        consume(2 * t + 1, s1_s, False)
        return 0

    lax.fori_loop(0, i // 2, pair, 0)

    @pl.when(i % 2 == 1)
    def _():
        scores(i, s1_s)
        consume(i - 1, s0_s, False)
        consume(i, s1_s, True)

    @pl.when(i % 2 == 0)
    def _():
        consume(i, s0_s, True)

    acc = acc_s[...]
    o = acc[:, :LANES] / acc[:, LANES:]
    oa = o[:tq] - _lambda(lam_ref) * o[tq:]
    o_ref[0] = (_rms(oa) * sn_ref[...] * (1.0 - LAM_INIT)).astype(BF16)


def _prompt_attn(q, kb, vb, kmeta, vmeta, lam, sub_norm, tq):
    b, s, _ = q.shape
    assert s % tq == 0
    return pl.pallas_call(
        functools.partial(_attn_kernel, tq=tq),
        grid=(b, H_A, s // tq),
        in_specs=[pl.BlockSpec((1, tq, LANES), lambda b, h, i: (b, i, h)),
                  pl.BlockSpec((1, s, LANES), lambda b, h, i: (b, 0, h)),
                  pl.BlockSpec((1, s, LANES), lambda b, h, i: (b, 0, h)),
                  pl.BlockSpec((LANES, LANES), lambda b, h, i: (0, h)),
                  pl.BlockSpec((LANES, LANES), lambda b, h, i: (0, h)),
                  pl.BlockSpec((4, D_HEAD), lambda b, h, i: (0, 0)),
                  pl.BlockSpec((1, LANES), lambda b, h, i: (0, 0))],
        out_specs=pl.BlockSpec((1, tq, LANES), lambda b, h, i: (b, i, h)),
        out_shape=jax.ShapeDtypeStruct((b, s, W_A), BF16),
        scratch_shapes=[pltpu.VMEM((2 * tq, LANES), BF16),
                        pltpu.VMEM((2 * tq, tq), F32), pltpu.VMEM((2 * tq, tq), F32),
                        pltpu.VMEM((2 * tq, LANES), F32), pltpu.VMEM((2 * tq, 2 * LANES), F32)],
        compiler_params=_params(3),
        name="prompt_attn",
    )(q, kb, vb, kmeta, vmeta, lam, sub_norm)


def _mlstm_kernel(u_ref, bv_ref, gc_ref, gr_ref, cw_ref, cb_ref, c0_ref, n0_ref, m0_ref, t0_ref,
                  hb_ref, c_ref, n_ref, m_ref, t_ref, ue_ref, *, el, nb):
    c = pl.program_id(0)

    @pl.when(c == 0)
    def _():
        c_ref[...] = c0_ref[...]
        n_ref[...] = n0_ref[...]
        m_ref[...] = m0_ref[...]
        t_ref[...] = t0_ref[...]

    ri = lax.broadcasted_iota(jnp.int32, (el, el), 0)
    ci = lax.broadcasted_iota(jnp.int32, (el, el), 1)
    causal = ci <= ri
    tri = jnp.where(causal, 1.0, 0.0).astype(BF16)
    tri_t = jnp.where(ri <= ci, 1.0, 0.0).astype(BF16)
    lane = lax.broadcasted_iota(jnp.int32, (el, LANES), 1)

    qk, gc, gr, fh_cols, fh_rows = [], [], [], [], []
    for bi in range(nb):
        ue_ref[bi, 0:8, :] = t_ref[bi]
        ue_ref[bi, 8:8 + el, :] = u_ref[bi]
        conv = cb_ref[...] + ue_ref[bi, 5:5 + el, :] * cw_ref[0:1, :]
        for j in range(1, CONV_W):
            conv = conv + ue_ref[bi, 5 + j:5 + j + el, :] * cw_ref[j:j + 1, :]
        t_ref[bi] = ue_ref[bi, el:el + 8, :]
        qk.append(conv * _sigmoid(conv))
        gc.append(gc_ref[bi])
        gr.append(gr_ref[bi])
        g1, g2, g3 = _split3(gc[bi])
        fh_cols.append(_dot(tri, g1) + _dot(tri, g2) + _dot(tri, g3))
        r1, r2, r3 = _split3(gr[bi])
        fh_rows.append(_dot(r1, tri_t) + _dot(r2, tri_t) + _dot(r3, tri_t))

    groups = [(bi, h) for bi in range(nb) for h in range(H_B)]
    mine = [(lane >= DQK) if (h % 2) else (lane < DQK) for h in range(H_B)]
    qp = lambda bi, h: qk[bi][:, (h // 2) * LANES:(h // 2 + 1) * LANES]
    kp = lambda bi, h: qk[bi][:, H_B * DQK + (h // 2) * LANES:H_B * DQK + (h // 2 + 1) * LANES] * (DQK ** -0.5)
    qm = jnp.stack([jnp.where(mine[h], qp(bi, h), 0.0) for bi, h in groups])
    km = jnp.stack([jnp.where(mine[h], kp(bi, h), 0.0) for bi, h in groups])
    kpb = jnp.stack([kp(bi, h) for bi, h in groups]).astype(BF16)
    v3 = jnp.stack([bv_ref[bi, :, h * DV:(h + 1) * DV] for bi, h in groups])
    rep = lambda col: jnp.broadcast_to(col, (el, LANES))
    fhc = jnp.stack([rep(fh_cols[bi][:, H_B + h:H_B + h + 1]) for bi, h in groups])
    igc = jnp.stack([rep(gc[bi][:, h:h + 1]) for bi, h in groups])
    fhr = jnp.stack([fh_rows[bi][H_B + h:H_B + h + 1, :] for bi, h in groups])
    igr = jnp.stack([gr[bi][h:h + 1, :] for bi, h in groups])
    m0 = jnp.stack([m_ref[bi, h:h + 1, :] for bi, h in groups])
    n0 = jnp.stack([n_ref[bi, h:h + 1, :] for bi, h in groups])
    c0 = c_ref[...].reshape(nb * H_B, DV, LANES)

    dmat = jnp.where(causal[None], fhc[:, :, :el] - fhr + igr, NEG_INF)
    b_in = fhc + m0
    m = jnp.maximum(b_in, jnp.max(dmat, axis=2, keepdims=True))
    wts = jnp.exp(dmat - m[:, :, :el])
    inter = jnp.exp(b_in - m)
    qmb = qm.astype(BF16)
    s = (_bdot(qmb, kpb, 2, 2) * wts).astype(BF16)
    sv = _bdot(s, jnp.concatenate([v3, jnp.ones_like(v3)], axis=2).astype(BF16), 2, 1)
    qc = _bdot(qmb, jnp.concatenate([c0, jnp.broadcast_to(n0, c0.shape)], axis=1).astype(BF16), 2, 2)
    num = sv[:, :, :DV] + inter * qc[:, :, :DV]
    den = sv[:, :, DV:] + inter * qc[:, :, DV:]
    hn = _rms(num / jnp.maximum(jnp.abs(den), jnp.exp(-m)))

    m_new = m[:, el - 1:el, :]
    fl = fhc[:, el - 1:el, :]
    wl = jnp.exp(fl - fhc + igc - m_new)
    decay = jnp.exp(fl + m0 - m_new)
    c_new = decay * c0 + _bdot((v3 * wl).astype(BF16), km.astype(BF16), 1, 1)
    n_new = decay * n0 + jnp.sum(wl * km, axis=1, keepdims=True)
    c_ref[...] = c_new.reshape(nb, H_B, DV, LANES)
    for g, (bi, h) in enumerate(groups):
        hb_ref[bi, :, h * DV:(h + 1) * DV] = hn[g]
        n_ref[bi, h:h + 1, :] = n_new[g]
        m_ref[bi, h:h + 1, :] = m_new[g]


def _mlstm(u, bv, gc, gr, conv_w, conv_b, c0, n0, m0, t0, el):
    b, s, _ = u.shape
    nc = s // el
    assert s % el == 0 and gr.shape == (b, 8, s)
    st = lambda shape: pl.BlockSpec((b,) + shape, lambda c: (0,) * (len(shape) + 1))
    seq = lambda w: pl.BlockSpec((b, el, w), lambda c: (0, c, 0))
    outs = [jax.ShapeDtypeStruct((b, s, W_BV), F32),
            jax.ShapeDtypeStruct((b, H_B, DV, LANES), F32),
            jax.ShapeDtypeStruct((b, H_B, LANES), F32),
            jax.ShapeDtypeStruct((b, H_B, LANES), F32),
            jax.ShapeDtypeStruct((b, 8, W_U), F32)]
    return pl.pallas_call(
        functools.partial(_mlstm_kernel, el=el, nb=b),
        grid=(nc,),
        in_specs=[seq(W_U), seq(W_BV), seq(LANES),
                  pl.BlockSpec((b, 8, el), lambda c: (0, 0, c)),
                  pl.BlockSpec((CONV_W, W_U), lambda c: (0, 0)),
                  pl.BlockSpec((1, W_U), lambda c: (0, 0)),
                  st((H_B, DV, LANES)), st((H_B, LANES)), st((H_B, LANES)), st((8, W_U))],
        out_specs=[seq(W_BV), st((H_B, DV, LANES)), st((H_B, LANES)), st((H_B, LANES)), st((8, W_U))],
        out_shape=outs,
        scratch_shapes=[pltpu.VMEM((b, el + 8, W_U), F32)],
        compiler_params=_params(1),
        name="mlstm_scan",
    )(u, bv, gc, gr, conv_w, conv_b, c0, n0, m0, t0)


def _mixer_kernel(x_ref, ya_ref, hb_ref, g_ref, wg_ref, bg_ref, wbo_ref, mh_ref, pa_ref, pb_ref, wo_ref, o_ref):
    x = x_ref[...]
    d = x.shape[1]
    h = (_rms(x) * g_ref[...]).astype(BF16)
    bo = _dot(h, wbo_ref[...])
    yb = (hb_ref[...] * mh_ref[...] * _sigmoid(bo)).astype(BF16)
    pa = _dot(ya_ref[...], pa_ref[...])
    pb = _dot(yb, pb_ref[...])
    ga = _sigmoid(_dot(h, wg_ref[:, :d]) + bg_ref[:, :d])
    gb = _sigmoid(_dot(h, wg_ref[:, d:]) + bg_ref[:, d:])
    merged = (ga * pa + gb * pb).astype(BF16)
    o_ref[...] = x + _dot(merged, wo_ref[...])


def _mixer(x, ya, hbn, mix_norm, wp, tm):
    m, d = x.shape
    assert m % tm == 0
    row = lambda w: pl.BlockSpec((tm, w), lambda i: (i, 0))
    return pl.pallas_call(
        _mixer_kernel,
        grid=(m // tm,),
        in_specs=[row(d), row(W_A), row(W_BV), _const_spec((1, d)),
                  _const_spec((d, 2 * d)), _const_spec((1, 2 * d)), _const_spec((d, W_BV)),
                  _const_spec((1, W_BV)), _const_spec((W_A, d)), _const_spec((W_BV, d)),
                  _const_spec((d, d))],
        out_specs=row(d),
        out_shape=jax.ShapeDtypeStruct((m, d), F32),
        compiler_params=_params(1),
        name="mixer_out",
    )(x, ya, hbn, mix_norm, wp["wg"], wp["bg"], wp["wbo"], wp["mh"], wp["pa"], wp["pb"], wp["wo"])


def _decode_kernel(pt_ref, qrow_ref, q_ref, kn_ref, vn_ref, lam_ref, sn_ref, *refs, pps):
    k_refs = refs[:pps]
    v_refs = refs[pps:2 * pps]
    o_ref, m_s, l_s, acc_s = refs[2 * pps:]
    g = pl.program_id(1)
    ng = pl.num_programs(1)
    nsub = 2 * H_A
    page = k_refs[0].shape[1]

    @pl.when(g == 0)
    def _():
        m_s[...] = jnp.full(m_s.shape, NEG_INF, F32)
        l_s[...] = jnp.zeros(l_s.shape, F32)
        acc_s[...] = jnp.zeros(acc_s.shape, F32)

    row = lax.broadcasted_iota(jnp.int32, (nsub, W_A), 0)
    lane = lax.broadcasted_iota(jnp.int32, (nsub, W_A), 1)
    qbd = jnp.where(lane // D_HEAD == row, jnp.broadcast_to(qrow_ref[0].astype(F32), (nsub, W_A)), 0.0).astype(BF16)

    s = jnp.concatenate([_dot(qbd, k_refs[i][...].astype(BF16)) for i in range(pps)], axis=1)
    m = m_s[...]
    m_new = jnp.maximum(m, jnp.max(s, axis=1, keepdims=True))
    alpha = jnp.exp2(m - m_new)
    p = jnp.exp2(s - m_new)
    l_s[...] = alpha * l_s[...] + jnp.sum(p, axis=1, keepdims=True)
    m_s[...] = m_new
    pb = p.astype(BF16)
    for h in range(H_A):
        pv = _dot(pb[:, 0:page], v_refs[0][pl.ds(h, page, stride=H_A), :].astype(BF16))
        for i in range(1, pps):
            pv = pv + _dot(pb[:, i * page:(i + 1) * page], v_refs[i][pl.ds(h, page, stride=H_A), :].astype(BF16))
        acc_s[h] = alpha * acc_s[h] + pv

    @pl.when(g == ng - 1)
    def _():
        q8 = q_ref[0].astype(F32)
        s_self = jnp.sum(q8 * kn_ref[0].astype(BF16).astype(F32), axis=1, keepdims=True)
        m = m_s[...]
        m_new = jnp.maximum(m, s_self)
        alpha = jnp.exp2(m - m_new)
        p_self = jnp.exp2(s_self - m_new)
        l = alpha * l_s[...] + p_self
        lam = _lambda(lam_ref)
        vn = vn_ref[0].astype(BF16).astype(F32)
        outs = []
        for h in range(H_A):
            o = (alpha * acc_s[h] + p_self * vn[h:h + 1, :]) / l
            oa = o[2 * h:2 * h + 1, :] - lam * o[2 * h + 1:2 * h + 2, :]
            outs.append(_rms(oa) * sn_ref[...] * (1.0 - LAM_INIT))
        o_ref[0] = jnp.concatenate(outs, axis=1).astype(BF16)


def _decode_attn(page_table, q, k_new, v_new, cache_k, cache_v, lam, sub_norm, pps):
    b, n_pages = page_table.shape
    n_pool, page = cache_k.shape[1], cache_k.shape[2]
    nsub = 2 * H_A
    assert n_pages % pps == 0 and cache_k.shape[0] == 1
    pt = page_table.reshape(-1)
    ck = jnp.transpose(cache_k[0], (0, 2, 3, 1)).reshape(n_pool, nsub * D_HEAD, page)
    cv = cache_v[0].reshape(n_pool, page * H_A, 2 * D_HEAD)

    kspec = lambda i: pl.BlockSpec((None, nsub * D_HEAD, page),
                                   lambda b, g, pt: (pt[b * n_pages + g * pps + i], 0, 0))
    vspec = lambda i: pl.BlockSpec((None, page * H_A, 2 * D_HEAD),
                                   lambda b, g, pt: (pt[b * n_pages + g * pps + i], 0, 0))
    tok = lambda heads, width: pl.BlockSpec((1, heads, width), lambda b, g, pt: (b, 0, 0))
    grid_spec = pltpu.PrefetchScalarGridSpec(
        num_scalar_prefetch=1,
        grid=(b, n_pages // pps),
        in_specs=[tok(1, W_A), tok(nsub, D_HEAD), tok(nsub, D_HEAD), tok(H_A, 2 * D_HEAD),
                  pl.BlockSpec((4, D_HEAD), lambda b, g, pt: (0, 0)),
                  pl.BlockSpec((1, LANES), lambda b, g, pt: (0, 0))]
                 + [kspec(i) for i in range(pps)] + [vspec(i) for i in range(pps)],
        out_specs=pl.BlockSpec((1, 1, W_A), lambda b, g, pt: (b, 0, 0)),
        scratch_shapes=[pltpu.VMEM((nsub, 1), F32), pltpu.VMEM((nsub, 1), F32),
                        pltpu.VMEM((H_A, nsub, 2 * D_HEAD), F32)],
    )
    return pl.pallas_call(
        functools.partial(_decode_kernel, pps=pps),
        grid_spec=grid_spec,
        out_shape=jax.ShapeDtypeStruct((b, 1, W_A), BF16),
        compiler_params=_params(2),
        name="decode_attn",
    )(pt, q.reshape(b, 1, W_A), q, k_new, v_new, lam, sub_norm, *([ck] * pps), *([cv] * pps))


def _mlstm_step_kernel(u_ref, cs_ref, bv_ref, gc_ref, cw_ref, cb_ref, c0_ref, n0_ref, m0_ref,
                       hb_ref, c_ref, n_ref, m_ref, cso_ref, *, nb):
    u = u_ref[...]
    cs = cs_ref[...]
    conv = cb_ref[...] + u * cw_ref[CONV_W - 1:CONV_W, :]
    for j in range(CONV_W - 1):
        conv = conv + cs[:, j * W_U:(j + 1) * W_U] * cw_ref[j:j + 1, :]
    cso_ref[...] = jnp.concatenate([cs[:, W_U:], u], axis=1)
    qk = conv * _sigmoid(conv)
    gc = gc_ref[...]
    bv = bv_ref[...]
    m0_all = m0_ref[...]
    ident = jnp.where(lax.broadcasted_iota(jnp.int32, (DV, DV), 0) == lax.broadcasted_iota(jnp.int32, (DV, DV), 1),
                      1.0, 0.0).astype(BF16)
    m_cols = []
    for h in range(H_B):
        qh = qk[:, h * DQK:(h + 1) * DQK]
        kh = qk[:, H_B * DQK + h * DQK:H_B * DQK + (h + 1) * DQK] * (DQK ** -0.5)
        vh = bv[:, h * DV:(h + 1) * DV]
        ig = gc[:, h:h + 1]
        logf = gc[:, H_B + h:H_B + h + 1]
        m0 = m0_all[:, h:h + 1]
        n0 = n0_ref[:, h, :]
        b_in = logf + m0
        m = jnp.maximum(b_in, ig)
        wts = jnp.exp(ig - m)
        inter = jnp.exp(b_in - m)
        s = jnp.sum(qh * kh, axis=1, keepdims=True) * wts
        qhb = qh.astype(BF16)
        cq_rows = []
        for b in range(nb):
            cq_rows.append(_nt(qhb, c0_ref[b, h].astype(BF16))[b:b + 1, :])
        cq = jnp.concatenate(cq_rows, axis=0)
        num = s * vh + inter * cq
        den = s + inter * jnp.sum(n0 * qh, axis=1, keepdims=True)
        hout = num / jnp.maximum(jnp.abs(den), jnp.exp(-m))
        hb_ref[:, h * DV:(h + 1) * DV] = _rms(hout)
        vcols = _nt(ident, (vh * wts).astype(BF16))
        for b in range(nb):
            c_ref[b, h] = inter[b:b + 1, :] * c0_ref[b, h] + vcols[:, b:b + 1] * kh[b:b + 1, :]
        n_ref[:, h, :] = inter * n0 + wts * kh
        m_cols.append(m)
    m_ref[...] = jnp.concatenate(m_cols, axis=1)


def _mlstm_step(u, conv_state, bv, gc, conv_w, conv_b, c0, n0, m0, nb):
    b = u.shape[0]
    assert b % nb == 0
    row = lambda w: pl.BlockSpec((nb, w), lambda i: (i, 0))
    outs = [jax.ShapeDtypeStruct((b, W_BV), F32),
            jax.ShapeDtypeStruct((b, H_B, DV, DQK), F32),
            jax.ShapeDtypeStruct((b, H_B, DQK), F32),
            jax.ShapeDtypeStruct((b, H_B), F32),
            jax.ShapeDtypeStruct((b, (CONV_W - 1) * W_U), F32)]
    cspec = pl.BlockSpec((nb, H_B, DV, DQK), lambda i: (i, 0, 0, 0))
    nspec = pl.BlockSpec((nb, H_B, DQK), lambda i: (i, 0, 0))
    return pl.pallas_call(
        functools.partial(_mlstm_step_kernel, nb=nb),
        grid=(b // nb,),
        in_specs=[row(W_U), row((CONV_W - 1) * W_U), row(W_BV), row(LANES),
                  pl.BlockSpec((CONV_W, W_U), lambda i: (0, 0)),
                  pl.BlockSpec((1, W_U), lambda i: (0, 0)),
                  cspec, nspec, row(H_B)],
        out_specs=[row(W_BV), cspec, nspec, row(H_B), row((CONV_W - 1) * W_U)],
        out_shape=outs,
        compiler_params=_params(1),
        name="mlstm_step",
    )(u, conv_state, bv, gc, conv_w, conv_b, c0, n0, m0)


def _rope_tables(pos):
    half = D_HEAD // 2
    inv = 1.0 / (ROPE_THETA ** (jnp.arange(half, dtype=F32) / half))
    ang = pos.astype(F32)[:, None] * inv[None, :]
    cos = jnp.cos(ang)
    sin = jnp.sin(ang)
    return jnp.tile(cos, (1, 4)), jnp.tile(jnp.concatenate([-sin, sin], axis=1), (1, 2))


def kernel(x_prompt, x_sample, cache_k, cache_v, page_table, state_C, state_n, state_m, state_conv, meta_tokens, ffn1_norm, ffn1_w_in, ffn1_w_out, mix_norm, w_in, b_if, b_gate, q_norm, k_norm, lam_q1, lam_k1, lam_q2, lam_k2, sub_norm, conv_w, conv_b, mh_norm, p_a, p_b, w_o, ffn2_norm, ffn2_w_in, ffn2_w_out):
    bp, s, d = x_prompt.shape
    bs = x_sample.shape[0]
    assert x_sample.shape[1] == 1 and w_in.shape[0] == 1
    n_pages = page_table.shape[1]
    page = cache_k.shape[2]

    w = w_in[0]
    o = 0
    cols = {}
    for name, width in (("q", W_A), ("k", W_A), ("v", W_A), ("u", W_U), ("bv", W_BV), ("bo", W_BV),
                        ("if", 2 * H_B), ("g", 2 * d)):
        cols[name] = w[:, o:o + width]
        o += width
    gidx = jnp.arange(2 * LANES) // D_HEAD
    wp = dict(
        wqk=jnp.concatenate([cols["q"], cols["k"]], axis=1).astype(BF16),
        wv=cols["v"].astype(BF16), wu=cols["u"].astype(BF16), wbv=cols["bv"].astype(BF16),
        wif=jnp.pad(cols["if"], ((0, 0), (0, LANES - 2 * H_B))).astype(BF16),
        bif=jnp.pad(b_if[0], (0, LANES - 2 * H_B)).reshape(1, LANES),
        qn=jnp.tile(q_norm[0], LANES // D_HEAD).reshape(1, LANES),
        kn=jnp.tile(k_norm[0], LANES // D_HEAD).reshape(1, LANES),
        gmat=jnp.where(gidx[:, None] == gidx[None, :], 1.0 / D_HEAD, 0.0).astype(BF16),
        wg=cols["g"].astype(BF16), bg=b_gate[0].reshape(1, 2 * d), wbo=cols["bo"].astype(BF16),
        mh=mh_norm[0].reshape(1, W_BV), pa=p_a[0].astype(BF16), pb=p_b[0].astype(BF16),
        wo=w_o[0].astype(BF16),
    )
    f1 = (ffn1_norm[0].reshape(1, d), ffn1_w_in[0].astype(BF16), ffn1_w_out[0].astype(BF16))
    f2 = (ffn2_norm[0].reshape(1, d), ffn2_w_in[0].astype(BF16), ffn2_w_out[0].astype(BF16))
    mixn = mix_norm[0].reshape(1, d)
    lam = jnp.stack([lam_q1[0], lam_k1[0], lam_q2[0], lam_k2[0]])
    subn = sub_norm[0].reshape(1, 2 * D_HEAD)
    cw, cb = conv_w[0], conv_b[0].reshape(1, W_U)

    tm = 512 if (bp * s) % 512 == 0 else 256
    xr = x_prompt.reshape(bp * s, d)
    n_small = -(-(bs + N_META) // 256) * 256
    xs = jnp.concatenate([x_sample.reshape(bs, d), meta_tokens.astype(F32),
                          jnp.zeros((n_small - bs - N_META, d), F32)], axis=0)
    past = n_pages * page
    pos_small = jnp.concatenate([jnp.full((bs,), past, jnp.int32), jnp.arange(N_META, dtype=jnp.int32),
                                 jnp.zeros((n_small - bs - N_META,), jnp.int32)])
    cos_r, sin_r = _rope_tables(N_META + jnp.arange(s))
    cos_s, sin_s = _rope_tables(pos_small)

    x1r = _ffn(xr, *f1, tm)
    x1s = _ffn(xs, *f1, n_small)
    q_r, kt_r, kb_r, v4_r, vb_r, u_r, bv_r, gc_r, gr_r = _proj(x1r, mixn, wp, cos_r, sin_r, tm, s // tm)
    q_s, kt_s, _, v4_s, _, u_s, bv_s, gc_s, gr_s = _proj(x1s, mixn, wp, cos_s, sin_s, n_small, 1)

    ssl = slice(0, bs)
    msl = slice(bs, bs + N_META)
    kt_meta = kt_s[0, :, msl]
    v4_meta = v4_s[bs * H_A:(bs + N_META) * H_A]
    k_new = kt_s[0, :, ssl].T.reshape(bs, 2 * H_A, D_HEAD)
    v_new = v4_s[:bs * H_A].reshape(bs, H_A, 2 * D_HEAD)
    k_meta, v_meta = kt_meta.T, v4_meta.reshape(N_META, W_A)
    pad_meta = lambda a: jnp.pad(a, ((0, LANES - N_META), (0, 0))).astype(BF16)
    _, c_m, n_m, m_m, t_m = _mlstm(
        u_s[msl][None], bv_s[msl][None], gc_s[msl][None], gr_s[:, msl][None], cw, cb,
        jnp.zeros((1, H_B, DV, LANES), F32), jnp.zeros((1, H_B, LANES), F32),
        jnp.full((1, H_B, LANES), NEG_INF, F32), jnp.zeros((1, 8, W_U), F32), N_META)

    tq = 512 if s % 512 == 0 else 256
    ya_r =_prompt_attn(q_r.reshape(bp, s, W_A), kb_r.reshape(bp, s, W_A), vb_r.reshape(bp, s, W_A),
                        pad_meta(k_meta), pad_meta(v_meta), lam, subn, tq)
    rep = lambda a: jnp.broadcast_to(a, (bp,) + a.shape[1:])
    hb_r, c_p, n_p, m_p, _ = _mlstm(
        u_r.reshape(bp, s, W_U), bv_r.reshape(bp, s, W_BV), gc_r.reshape(bp, s, LANES),
        jnp.swapaxes(gr_r.reshape(8, bp, s), 0, 1), cw, cb,
        rep(c_m), rep(n_m), rep(m_m), rep(t_m), CHUNK)

    pps = 32 if n_pages % 32 == 0 else n_pages
    ya_s = _decode_attn(page_table, q_s[ssl].reshape(bs, 2 * H_A, D_HEAD), k_new, v_new,
                        cache_k, cache_v, lam, subn, pps)
    hb_s, c_s, n_s, m_s, conv_s = _mlstm_step(
        u_s[ssl], state_conv[0].reshape(bs, (CONV_W - 1) * W_U), bv_s[ssl], gc_s[ssl], cw, cb,
        state_C[0], state_n[0], state_m[0], 8)

    y_r = _ffn(_mixer(x1r, ya_r.reshape(bp * s, W_A), hb_r.reshape(bp * s, W_BV), mixn, wp, tm), *f2, tm)
    y_s = _ffn(_mixer(x1s[ssl], ya_s.reshape(bs, W_A), hb_s, mixn, wp, bs), *f2, bs)

    def unpad_heads(a):
        return jnp.stack([a[:, h, ..., (h % 2) * DQK:(h % 2 + 1) * DQK] for h in range(H_B)], axis=1)

    kt_all = jnp.concatenate([jnp.broadcast_to(kt_meta[None], (bp, W_A, N_META)), kt_r], axis=2)
    k_prompt = jnp.transpose(kt_all.reshape(bp, 2 * H_A, D_HEAD, N_META + s), (0, 3, 1, 2))[None]
    v4_all = jnp.concatenate([jnp.broadcast_to(v4_meta[None], (bp, N_META * H_A, 2 * D_HEAD)),
                              v4_r.reshape(bp, s * H_A, 2 * D_HEAD)], axis=1)
    v_prompt = v4_all.reshape(1, bp, N_META + s, H_A, 2 * D_HEAD)
    return (y_r.reshape(bp, s, d), y_s.reshape(bs, 1, d), k_prompt, v_prompt,
            k_new.reshape(1, bs, 1, 2 * H_A, D_HEAD), v_new.reshape(1, bs, 1, H_A, 2 * D_HEAD),
            unpad_heads(c_p)[None], unpad_heads(n_p[:, :, None, :])[:, :, 0][None], m_p[:, :, 0][None],
            u_r.reshape(bp, s, W_U)[:, s - (CONV_W - 1):][None],
            c_s[None], n_s[None], m_s[None], conv_s.reshape(1, bs, CONV_W - 1, W_U))
```

```python
import functools

import jax
import jax.numpy as jnp
from jax import lax
from jax.experimental import pallas as pl
from jax.experimental.pallas import tpu as pltpu

F32 = jnp.float32
BF16 = jnp.bfloat16

N_META = 16
H_A = 4
D_HEAD = 64
H_B = 4
DQK = 64
DV = 128
CONV_W = 4
CHUNK = 128
ROPE_THETA = 10000.0
EPS = 1e-6
ATTN_SCALE = D_HEAD ** -0.5
Q_SCALE = ATTN_SCALE * 1.4426950408889634
LAM_INIT = 0.2
LANES = 128
W_A = 2 * H_A * D_HEAD
W_U = 2 * H_B * DQK
W_BV = H_B * DV
VMEM_LIMIT = 56 * 1024 * 1024
NEG_INF = float("-inf")


def _dot(a, b):
    return jnp.dot(a, b, preferred_element_type=F32)


def _nt(a, b):
    return lax.dot_general(a, b, (((1,), (1,)), ((), ())), preferred_element_type=F32)


def _bdot(a, b, ca, cb):
    return lax.dot_general(a, b, (((ca,), (cb,)), ((0,), (0,))), preferred_element_type=F32)


def _rms(x):
    return x * lax.rsqrt(jnp.mean(x * x, axis=-1, keepdims=True) + EPS)


def _sigmoid(x):
    return 1.0 / (1.0 + jnp.exp(-x))


def _log_sigmoid(x):
    return jnp.minimum(x, 0.0) - jnp.log1p(jnp.exp(-jnp.abs(x)))


def _split3(x):
    x1 = x.astype(BF16)
    r = x - x1.astype(F32)
    x2 = r.astype(BF16)
    x3 = (r - x2.astype(F32)).astype(BF16)
    return x1, x2, x3


def _params(n_axes):
    return pltpu.CompilerParams(dimension_semantics=("arbitrary",) * n_axes,
                                vmem_limit_bytes=VMEM_LIMIT)


def _const_spec(shape):
    nd = len(shape)
    return pl.BlockSpec(shape, lambda *_: (0,) * nd, pipeline_mode=pl.Buffered(1))


def _ffn_kernel(x_ref, g_ref, win_ref, wout_ref, o_ref, *, d_ff, ck):
    x = x_ref[...]
    h = (_rms(x) * g_ref[...]).astype(BF16)
    acc = jnp.zeros(x.shape, F32)
    for c in range(d_ff // ck):
        g = _dot(h, win_ref[:, c * ck:(c + 1) * ck])
        u = _dot(h, win_ref[:, d_ff + c * ck:d_ff + (c + 1) * ck])
        a = (g * _sigmoid(g) * u).astype(BF16)
        acc = acc + _dot(a, wout_ref[c * ck:(c + 1) * ck, :])
    o_ref[...] = x + 0.5 * acc


def _ffn(x, norm_g, w_in, w_out, tm):
    m, d = x.shape
    d_ff = w_out.shape[0]
    ck = 256
    assert m % tm == 0 and d_ff % ck == 0
    return pl.pallas_call(
        functools.partial(_ffn_kernel, d_ff=d_ff, ck=ck),
        grid=(m // tm,),
        in_specs=[pl.BlockSpec((tm, d), lambda i: (i, 0)),
                  _const_spec((1, d)),
                  _const_spec((d, 2 * d_ff)),
                  _const_spec((d_ff, d))],
        out_specs=pl.BlockSpec((tm, d), lambda i: (i, 0)),
        out_shape=jax.ShapeDtypeStruct((m, d), F32),
        compiler_params=_params(1),
        name="ffn",
    )(x, norm_g, w_in, w_out)


def _proj_kernel(x_ref, g_ref, wqk_ref, wv_ref, wu_ref, wbv_ref, wif_ref, bif_ref,
                 qn_ref, kn_ref, cos_ref, sin_ref, gmat_ref,
                 q_ref, kt_ref, kb_ref, v4_ref, vb_ref, u_ref, bv_ref, gc_ref, gr_ref):
    x = x_ref[...]
    tm = x.shape[0]
    h = (_rms(x) * g_ref[...]).astype(BF16)
    cos = cos_ref[...]
    sin = sin_ref[...]
    gmat = gmat_ref[...]
    lane = lax.broadcasted_iota(jnp.int32, (tm, LANES), 1)
    first_half = (lane % D_HEAD) < (D_HEAD // 2)

    def norm_rope(z2, nw):
        ms = _dot((z2 * z2).astype(BF16), gmat)
        y2 = z2 * lax.rsqrt(ms + EPS)
        outs = []
        for t in range(2):
            y = y2[:, t * LANES:(t + 1) * LANES] * nw
            rot = jnp.where(first_half, pltpu.roll(y, LANES - D_HEAD // 2, 1), pltpu.roll(y, D_HEAD // 2, 1))
            outs.append(y * cos + rot * sin)
        return outs

    qn = qn_ref[...]
    kn = kn_ref[...]
    zqk = _dot(h, wqk_ref[...])
    for pr in range(W_A // (2 * LANES)):
        c0 = pr * 2 * LANES
        for t, qq in enumerate(norm_rope(zqk[:, c0:c0 + 2 * LANES], qn)):
            q_ref[:, c0 + t * LANES:c0 + (t + 1) * LANES] = (qq * Q_SCALE).astype(BF16)
        for t, kk in enumerate(norm_rope(zqk[:, W_A + c0:W_A + c0 + 2 * LANES], kn)):
            kt_ref[c0 + t * LANES:c0 + (t + 1) * LANES, :] = kk.T
            kb_ref[:, c0 + t * LANES:c0 + (t + 1) * LANES] = kk.astype(BF16)
    v = _dot(h, wv_ref[...])
    for hd in range(H_A):
        v4_ref[pl.ds(hd, tm, stride=H_A), :] = v[:, hd * 2 * D_HEAD:(hd + 1) * 2 * D_HEAD]
    vb_ref[...] = v.astype(BF16)
    u_ref[...] = _dot(h, wu_ref[...])
    bv_ref[...] = _dot(h, wbv_ref[...])
    z = _dot(h, wif_ref[...]) + bif_ref[...]
    gc = jnp.where(lane < H_B, z, _log_sigmoid(z))
    gc_ref[...] = gc
    gr_ref[...] = gc.T[:8, :]


def _proj(x, mix_norm, wp, cos_t, sin_t, tm, pos_blocks):
    m, d = x.shape
    seq = pos_blocks * tm
    assert m % seq == 0
    row = lambda w: pl.BlockSpec((tm, w), lambda i: (i, 0))
    tab = pl.BlockSpec((tm, LANES), lambda i: (i % pos_blocks, 0))
    outs = [
        jax.ShapeDtypeStruct((m, W_A), BF16),
        jax.ShapeDtypeStruct((m // seq, W_A, seq), F32),
        jax.ShapeDtypeStruct((m, W_A), BF16),
        jax.ShapeDtypeStruct((m * H_A, 2 * D_HEAD), F32),
        jax.ShapeDtypeStruct((m, W_A), BF16),
        jax.ShapeDtypeStruct((m, W_U), F32),
        jax.ShapeDtypeStruct((m, W_BV), F32),
        jax.ShapeDtypeStruct((m, LANES), F32),
        jax.ShapeDtypeStruct((8, m), F32),
    ]
    return pl.pallas_call(
        _proj_kernel,
        grid=(m // tm,),
        in_specs=[row(d), _const_spec((1, d)),
                  _const_spec((d, 2 * W_A)), _const_spec((d, W_A)), _const_spec((d, W_U)),
                  _const_spec((d, W_BV)), _const_spec((d, LANES)), _const_spec((1, LANES)),
                  _const_spec((1, LANES)), _const_spec((1, LANES)), tab, tab,
                  _const_spec((2 * LANES, 2 * LANES))],
        out_specs=[row(W_A),
                   pl.BlockSpec((None, W_A, tm), lambda i: (i // pos_blocks, 0, i % pos_blocks)),
                   row(W_A),
                   pl.BlockSpec((tm * H_A, 2 * D_HEAD), lambda i: (i, 0)),
                   row(W_A), row(W_U), row(W_BV),
                   row(LANES), pl.BlockSpec((8, tm), lambda i: (0, i))],
        out_shape=outs,
        compiler_params=_params(1),
        name="proj",
    )(x, mix_norm, wp["wqk"], wp["wv"], wp["wu"], wp["wbv"], wp["wif"], wp["bif"],
      wp["qn"], wp["kn"], cos_t, sin_t, wp["gmat"])


def _lambda(lam_ref):
    a = jnp.sum(lam_ref[0:1, :] * lam_ref[1:2, :], axis=-1, keepdims=True)
    b = jnp.sum(lam_ref[2:3, :] * lam_ref[3:4, :], axis=-1, keepdims=True)
    return jnp.exp(a) - jnp.exp(b) + LAM_INIT


def _attn_kernel(q_ref, k_ref, v_ref, km_ref, vm_ref, lam_ref, sn_ref, o_ref,
                 qq_s, s0_s, s1_s, m_s, acc_s, *, tq, hg):
    i = pl.program_id(2)
    lane = lax.broadcasted_iota(jnp.int32, (tq, LANES), 1)
    heads = lambda a: [a[:, g * LANES:(g + 1) * LANES] for g in range(hg)]
    for g, q in enumerate(heads(q_ref[0])):
        zero = jnp.zeros_like(q)
        qq_s[g, 0:tq, :] = jnp.where(lane < D_HEAD, q, zero)
        qq_s[g, tq:2 * tq, :] = jnp.where(lane >= D_HEAD, q, zero)

    def with_ones(v):
        return jnp.stack([jnp.concatenate([vh, jnp.ones_like(vh)], axis=1) for vh in heads(v)])

    def scores(j, s_ref):
        off = pl.multiple_of(j * tq, tq)
        s_ref[...] = _bdot(qq_s[...], jnp.stack(heads(k_ref[0, pl.ds(off, tq), :])), 2, 2)

    def consume(j, s_ref, masked):
        off = pl.multiple_of(j * tq, tq)
        s = s_ref[...]
        if masked:
            row = lax.broadcasted_iota(jnp.int32, s.shape, 1)
            colk = lax.broadcasted_iota(jnp.int32, s.shape, 2)
            qidx = jnp.where(row >= tq, row - tq, row)
            s = jnp.where(colk <= qidx, s, NEG_INF)
        m = m_s[...]
        m_new = jnp.maximum(m, jnp.max(s, axis=2, keepdims=True))
        p = jnp.exp2(s - jnp.concatenate([m_new] * (tq // LANES), axis=2)).astype(BF16)
        alpha = jnp.exp2(m - m_new)
        acc_s[...] = (jnp.concatenate([alpha, alpha], axis=2) * acc_s[...]
                      + _bdot(p, with_ones(v_ref[0, pl.ds(off, tq), :]), 2, 1))
        m_s[...] = m_new

    s = _bdot(qq_s[...], jnp.stack(heads(km_ref[...])), 2, 2)
    col = lax.broadcasted_iota(jnp.int32, s.shape, 2)
    s = jnp.where(col < N_META, s, NEG_INF)
    m = jnp.max(s, axis=2, keepdims=True)
    m_s[...] = jnp.broadcast_to(m, (hg, 2 * tq, LANES))
    acc_s[...] = _bdot(jnp.exp2(s - m).astype(BF16), with_ones(vm_ref[...]), 2, 1)

    scores(0, s0_s)

    def pair(t, _):
        scores(2 * t + 1, s1_s)
        consume(2 * t, s0_s, False)
        scores(2 * t + 2, s0_s)
        consume(2 * t + 1, s1_s, False)
        return 0

    lax.fori_loop(0, i // 2, pair, 0)

    @pl.when(i % 2 == 1)
    def _():
        scores(i, s1_s)
        consume(i - 1, s0_s, False)
        consume(i, s1_s, True)

    @pl.when(i % 2 == 0)
    def _():
        consume(i, s0_s, True)

    acc = acc_s[...]
    o = acc[:, :, :LANES] / acc[:, :, LANES:]
    oa = o[:, :tq] - _lambda(lam_ref) * o[:, tq:]
    y = (_rms(oa) * sn_ref[...] * (1.0 - LAM_INIT)).astype(BF16)
    for g in range(hg):
        o_ref[0, :, g * LANES:(g + 1) * LANES] = y[g]


def _prompt_attn(q, kb, vb, kmeta, vmeta, lam, sub_norm, tq, hg):
    b, s, _ = q.shape
    assert s % tq == 0 and H_A % hg == 0
    w = hg * LANES
    return pl.pallas_call(
        functools.partial(_attn_kernel, tq=tq, hg=hg),
        grid=(b, H_A // hg, s // tq),
        in_specs=[pl.BlockSpec((1, tq, w), lambda b, h, i: (b, i, h)),
                  pl.BlockSpec((1, s, w), lambda b, h, i: (b, 0, h)),
                  pl.BlockSpec((1, s, w), lambda b, h, i: (b, 0, h)),
                  pl.BlockSpec((LANES, w), lambda b, h, i: (0, h)),
                  pl.BlockSpec((LANES, w), lambda b, h, i: (0, h)),
                  pl.BlockSpec((4, D_HEAD), lambda b, h, i: (0, 0)),
                  pl.BlockSpec((1, LANES), lambda b, h, i: (0, 0))],
        out_specs=pl.BlockSpec((1, tq, w), lambda b, h, i: (b, i, h)),
        out_shape=jax.ShapeDtypeStruct((b, s, W_A), BF16),
        scratch_shapes=[pltpu.VMEM((hg, 2 * tq, LANES), BF16),
                        pltpu.VMEM((hg, 2 * tq, tq), F32), pltpu.VMEM((hg, 2 * tq, tq), F32),
                        pltpu.VMEM((hg, 2 * tq, LANES), F32), pltpu.VMEM((hg, 2 * tq, 2 * LANES), F32)],
        compiler_params=_params(3),
        name="prompt_attn",
    )(q, kb, vb, kmeta, vmeta, lam, sub_norm)


def _mlstm_kernel(u_ref, bv_ref, gc_ref, gr_ref, cw_ref, cb_ref, c0_ref, n0_ref, m0_ref, t0_ref,
                  hb_ref, c_ref, n_ref, m_ref, t_ref, ue_ref, *, el, nb):
    c = pl.program_id(0)

    @pl.when(c == 0)
    def _():
        c_ref[...] = c0_ref[...]
        n_ref[...] = n0_ref[...]
        m_ref[...] = m0_ref[...]
        t_ref[...] = t0_ref[...]

    ri = lax.broadcasted_iota(jnp.int32, (el, el), 0)
    ci = lax.broadcasted_iota(jnp.int32, (el, el), 1)
    causal = ci <= ri
    tri = jnp.where(causal, 1.0, 0.0).astype(BF16)
    tri_t = jnp.where(ri <= ci, 1.0, 0.0).astype(BF16)
    lane = lax.broadcasted_iota(jnp.int32, (el, LANES), 1)

    qk, gc, gr, fh_cols, fh_rows = [], [], [], [], []
    for bi in range(nb):
        ue_ref[bi, 0:8, :] = t_ref[bi]
        ue_ref[bi, 8:8 + el, :] = u_ref[bi]
        conv = cb_ref[...] + ue_ref[bi, 5:5 + el, :] * cw_ref[0:1, :]
        for j in range(1, CONV_W):
            conv = conv + ue_ref[bi, 5 + j:5 + j + el, :] * cw_ref[j:j + 1, :]
        t_ref[bi] = ue_ref[bi, el:el + 8, :]
        qk.append(conv * _sigmoid(conv))
        gc.append(gc_ref[bi])
        gr.append(gr_ref[bi])
        g1, g2, g3 = _split3(gc[bi])
        fh_cols.append(_dot(tri, g1) + _dot(tri, g2) + _dot(tri, g3))
        r1, r2, r3 = _split3(gr[bi])
        fh_rows.append(_dot(r1, tri_t) + _dot(r2, tri_t) + _dot(r3, tri_t))

    groups = [(bi, h) for bi in range(nb) for h in range(H_B)]
    mine = [(lane >= DQK) if (h % 2) else (lane < DQK) for h in range(H_B)]
    qp = lambda bi, h: qk[bi][:, (h // 2) * LANES:(h // 2 + 1) * LANES]
    kp = lambda bi, h: qk[bi][:, H_B * DQK + (h // 2) * LANES:H_B * DQK + (h // 2 + 1) * LANES] * (DQK ** -0.5)
    qm = jnp.stack([jnp.where(mine[h], qp(bi, h), 0.0) for bi, h in groups])
    km = jnp.stack([jnp.where(mine[h], kp(bi, h), 0.0) for bi, h in groups])
    kpb = jnp.stack([kp(bi, h) for bi, h in groups]).astype(BF16)
    v3 = jnp.stack([bv_ref[bi, :, h * DV:(h + 1) * DV] for bi, h in groups])
    rep = lambda col: jnp.broadcast_to(col, (el, LANES))
    fhc = jnp.stack([rep(fh_cols[bi][:, H_B + h:H_B + h + 1]) for bi, h in groups])
    igc = jnp.stack([rep(gc[bi][:, h:h + 1]) for bi, h in groups])
    fhr = jnp.stack([fh_rows[bi][H_B + h:H_B + h + 1, :] for bi, h in groups])
    igr = jnp.stack([gr[bi][h:h + 1, :] for bi, h in groups])
    m0 = jnp.stack([m_ref[bi, h:h + 1, :] for bi, h in groups])
    n0 = jnp.stack([n_ref[bi, h:h + 1, :] for bi, h in groups])
    c0 = c_ref[...].reshape(nb * H_B, DV, LANES)

    dmat = jnp.where(causal[None], fhc[:, :, :el] - fhr + igr, NEG_INF)
    b_in = fhc + m0
    m = jnp.maximum(b_in, jnp.max(dmat, axis=2, keepdims=True))
    wts = jnp.exp(dmat - m[:, :, :el])
    inter = jnp.exp(b_in - m)
    qmb = qm.astype(BF16)
    s = (_bdot(qmb, kpb, 2, 2) * wts).astype(BF16)
    sv = _bdot(s, jnp.concatenate([v3, jnp.ones_like(v3)], axis=2).astype(BF16), 2, 1)
    qc = _bdot(qmb, jnp.concatenate([c0, jnp.broadcast_to(n0, c0.shape)], axis=1).astype(BF16), 2, 2)
    num = sv[:, :, :DV] + inter * qc[:, :, :DV]
    den = sv[:, :, DV:] + inter * qc[:, :, DV:]
    hn = _rms(num / jnp.maximum(jnp.abs(den), jnp.exp(-m)))

    m_new = m[:, el - 1:el, :]
    fl = fhc[:, el - 1:el, :]
    wl = jnp.exp(fl - fhc + igc - m_new)
    decay = jnp.exp(fl + m0 - m_new)
    c_new = decay * c0 + _bdot((v3 * wl).astype(BF16), km.astype(BF16), 1, 1)
    n_new = decay * n0 + jnp.sum(wl * km, axis=1, keepdims=True)
    c_ref[...] = c_new.reshape(nb, H_B, DV, LANES)
    for g, (bi, h) in enumerate(groups):
        hb_ref[bi, :, h * DV:(h + 1) * DV] = hn[g]
        n_ref[bi, h:h + 1, :] = n_new[g]
        m_ref[bi, h:h + 1, :] = m_new[g]


def _mlstm(u, bv, gc, gr, conv_w, conv_b, c0, n0, m0, t0, el):
    b, s, _ = u.shape
    nc = s // el
    assert s % el == 0 and gr.shape == (b, 8, s)
    st = lambda shape: pl.BlockSpec((b,) + shape, lambda c: (0,) * (len(shape) + 1))
    seq = lambda w: pl.BlockSpec((b, el, w), lambda c: (0, c, 0))
    outs = [jax.ShapeDtypeStruct((b, s, W_BV), F32),
            jax.ShapeDtypeStruct((b, H_B, DV, LANES), F32),
            jax.ShapeDtypeStruct((b, H_B, LANES), F32),
            jax.ShapeDtypeStruct((b, H_B, LANES), F32),
            jax.ShapeDtypeStruct((b, 8, W_U), F32)]
    return pl.pallas_call(
        functools.partial(_mlstm_kernel, el=el, nb=b),
        grid=(nc,),
        in_specs=[seq(W_U), seq(W_BV), seq(LANES),
                  pl.BlockSpec((b, 8, el), lambda c: (0, 0, c)),
                  pl.BlockSpec((CONV_W, W_U), lambda c: (0, 0)),
                  pl.BlockSpec((1, W_U), lambda c: (0, 0)),
                  st((H_B, DV, LANES)), st((H_B, LANES)), st((H_B, LANES)), st((8, W_U))],
        out_specs=[seq(W_BV), st((H_B, DV, LANES)), st((H_B, LANES)), st((H_B, LANES)), st((8, W_U))],
        out_shape=outs,
        scratch_shapes=[pltpu.VMEM((b, el + 8, W_U), F32)],
        compiler_params=_params(1),
        name="mlstm_scan",
    )(u, bv, gc, gr, conv_w, conv_b, c0, n0, m0, t0)


def _mixer_kernel(x_ref, ya_ref, hb_ref, g_ref, wg_ref, bg_ref, wbo_ref, mh_ref, pa_ref, pb_ref, wo_ref, o_ref):
    x = x_ref[...]
    d = x.shape[1]
    h = (_rms(x) * g_ref[...]).astype(BF16)
    bo = _dot(h, wbo_ref[...])
    yb = (hb_ref[...] * mh_ref[...] * _sigmoid(bo)).astype(BF16)
    pa = _dot(ya_ref[...], pa_ref[...])
    pb = _dot(yb, pb_ref[...])
    ga = _sigmoid(_dot(h, wg_ref[:, :d]) + bg_ref[:, :d])
    gb = _sigmoid(_dot(h, wg_ref[:, d:]) + bg_ref[:, d:])
    merged = (ga * pa + gb * pb).astype(BF16)
    o_ref[...] = x + _dot(merged, wo_ref[...])


def _mixer(x, ya, hbn, mix_norm, wp, tm):
    m, d = x.shape
    assert m % tm == 0
    row = lambda w: pl.BlockSpec((tm, w), lambda i: (i, 0))
    return pl.pallas_call(
        _mixer_kernel,
        grid=(m // tm,),
        in_specs=[row(d), row(W_A), row(W_BV), _const_spec((1, d)),
                  _const_spec((d, 2 * d)), _const_spec((1, 2 * d)), _const_spec((d, W_BV)),
                  _const_spec((1, W_BV)), _const_spec((W_A, d)), _const_spec((W_BV, d)),
                  _const_spec((d, d))],
        out_specs=row(d),
        out_shape=jax.ShapeDtypeStruct((m, d), F32),
        compiler_params=_params(1),
        name="mixer_out",
    )(x, ya, hbn, mix_norm, wp["wg"], wp["bg"], wp["wbo"], wp["mh"], wp["pa"], wp["pb"], wp["wo"])


def _decode_kernel(pt_ref, qrow_ref, q_ref, kn_ref, vn_ref, lam_ref, sn_ref, *refs, pps):
    k_refs = refs[:pps]
    v_refs = refs[pps:2 * pps]
    o_ref, m_s, l_s, acc_s = refs[2 * pps:]
    g = pl.program_id(1)
    ng = pl.num_programs(1)
    nsub = 2 * H_A
    page = k_refs[0].shape[1]

    @pl.when(g == 0)
    def _():
        m_s[...] = jnp.full(m_s.shape, NEG_INF, F32)
        l_s[...] = jnp.zeros(l_s.shape, F32)
        acc_s[...] = jnp.zeros(acc_s.shape, F32)

    row = lax.broadcasted_iota(jnp.int32, (nsub, W_A), 0)
    lane = lax.broadcasted_iota(jnp.int32, (nsub, W_A), 1)
    qbd = jnp.where(lane // D_HEAD == row, jnp.broadcast_to(qrow_ref[0].astype(F32), (nsub, W_A)), 0.0).astype(BF16)

    s = jnp.concatenate([_dot(qbd, k_refs[i][...].astype(BF16)) for i in range(pps)], axis=1)
    m = m_s[...]
    m_new = jnp.maximum(m, jnp.max(s, axis=1, keepdims=True))
    alpha = jnp.exp2(m - m_new)
    p = jnp.exp2(s - m_new)
    l_s[...] = alpha * l_s[...] + jnp.sum(p, axis=1, keepdims=True)
    m_s[...] = m_new
    pb = p.astype(BF16)
    for h in range(H_A):
        pv = _dot(pb[:, 0:page], v_refs[0][pl.ds(h, page, stride=H_A), :].astype(BF16))
        for i in range(1, pps):
            pv = pv + _dot(pb[:, i * page:(i + 1) * page], v_refs[i][pl.ds(h, page, stride=H_A), :].astype(BF16))
        acc_s[h] = alpha * acc_s[h] + pv

    @pl.when(g == ng - 1)
    def _():
        q8 = q_ref[0].astype(F32)
        s_self = jnp.sum(q8 * kn_ref[0].astype(BF16).astype(F32), axis=1, keepdims=True)
        m = m_s[...]
        m_new = jnp.maximum(m, s_self)
        alpha = jnp.exp2(m - m_new)
        p_self = jnp.exp2(s_self - m_new)
        l = alpha * l_s[...] + p_self
        lam = _lambda(lam_ref)
        vn = vn_ref[0].astype(BF16).astype(F32)
        outs = []
        for h in range(H_A):
            o = (alpha * acc_s[h] + p_self * vn[h:h + 1, :]) / l
            oa = o[2 * h:2 * h + 1, :] - lam * o[2 * h + 1:2 * h + 2, :]
            outs.append(_rms(oa) * sn_ref[...] * (1.0 - LAM_INIT))
        o_ref[0] = jnp.concatenate(outs, axis=1).astype(BF16)


def _decode_attn(page_table, q, k_new, v_new, cache_k, cache_v, lam, sub_norm, pps):
    b, n_pages = page_table.shape
    n_pool, page = cache_k.shape[1], cache_k.shape[2]
    nsub = 2 * H_A
    assert n_pages % pps == 0 and cache_k.shape[0] == 1
    pt = page_table.reshape(-1)
    ck = jnp.transpose(cache_k[0], (0, 2, 3, 1)).reshape(n_pool, nsub * D_HEAD, page)
    cv = cache_v[0].reshape(n_pool, page * H_A, 2 * D_HEAD)

    kspec = lambda i: pl.BlockSpec((None, nsub * D_HEAD, page),
                                   lambda b, g, pt: (pt[b * n_pages + g * pps + i], 0, 0))
    vspec = lambda i: pl.BlockSpec((None, page * H_A, 2 * D_HEAD),
                                   lambda b, g, pt: (pt[b * n_pages + g * pps + i], 0, 0))
    tok = lambda heads, width: pl.BlockSpec((1, heads, width), lambda b, g, pt: (b, 0, 0))
    grid_spec = pltpu.PrefetchScalarGridSpec(
        num_scalar_prefetch=1,
        grid=(b, n_pages // pps),
        in_specs=[tok(1, W_A), tok(nsub, D_HEAD), tok(nsub, D_HEAD), tok(H_A, 2 * D_HEAD),
                  pl.BlockSpec((4, D_HEAD), lambda b, g, pt: (0, 0)),
                  pl.BlockSpec((1, LANES), lambda b, g, pt: (0, 0))]
                 + [kspec(i) for i in range(pps)] + [vspec(i) for i in range(pps)],
        out_specs=pl.BlockSpec((1, 1, W_A), lambda b, g, pt: (b, 0, 0)),
        scratch_shapes=[pltpu.VMEM((nsub, 1), F32), pltpu.VMEM((nsub, 1), F32),
                        pltpu.VMEM((H_A, nsub, 2 * D_HEAD), F32)],
    )
    return pl.pallas_call(
        functools.partial(_decode_kernel, pps=pps),
        grid_spec=grid_spec,
        out_shape=jax.ShapeDtypeStruct((b, 1, W_A), BF16),
        compiler_params=_params(2),
        name="decode_attn",
    )(pt, q.reshape(b, 1, W_A), q, k_new, v_new, lam, sub_norm, *([ck] * pps), *([cv] * pps))


def _mlstm_step_kernel(u_ref, cs_ref, bv_ref, gc_ref, cw_ref, cb_ref, c0_ref, n0_ref, m0_ref,
                       hb_ref, c_ref, n_ref, m_ref, cso_ref, *, nb):
    u = u_ref[...]
    cs = cs_ref[...]
    conv = cb_ref[...] + u * cw_ref[CONV_W - 1:CONV_W, :]
    for j in range(CONV_W - 1):
        conv = conv + cs[:, j * W_U:(j + 1) * W_U] * cw_ref[j:j + 1, :]
    cso_ref[...] = jnp.concatenate([cs[:, W_U:], u], axis=1)
    qk = conv * _sigmoid(conv)
    gc = gc_ref[...]
    bv = bv_ref[...]
    m0_all = m0_ref[...]
    ident = jnp.where(lax.broadcasted_iota(jnp.int32, (DV, DV), 0) == lax.broadcasted_iota(jnp.int32, (DV, DV), 1),
                      1.0, 0.0).astype(BF16)
    m_cols = []
    for h in range(H_B):
        qh = qk[:, h * DQK:(h + 1) * DQK]
        kh = qk[:, H_B * DQK + h * DQK:H_B * DQK + (h + 1) * DQK] * (DQK ** -0.5)
        vh = bv[:, h * DV:(h + 1) * DV]
        ig = gc[:, h:h + 1]
        logf = gc[:, H_B + h:H_B + h + 1]
        m0 = m0_all[:, h:h + 1]
        n0 = n0_ref[:, h, :]
        b_in = logf + m0
        m = jnp.maximum(b_in, ig)
        wts = jnp.exp(ig - m)
        inter = jnp.exp(b_in - m)
        s = jnp.sum(qh * kh, axis=1, keepdims=True) * wts
        qhb = qh.astype(BF16)
        cq_rows = []
        for b in range(nb):
            cq_rows.append(_nt(qhb, c0_ref[b, h].astype(BF16))[b:b + 1, :])
        cq = jnp.concatenate(cq_rows, axis=0)
        num = s * vh + inter * cq
        den = s + inter * jnp.sum(n0 * qh, axis=1, keepdims=True)
        hout = num / jnp.maximum(jnp.abs(den), jnp.exp(-m))
        hb_ref[:, h * DV:(h + 1) * DV] = _rms(hout)
        vcols = _nt(ident, (vh * wts).astype(BF16))
        for b in range(nb):
            c_ref[b, h] = inter[b:b + 1, :] * c0_ref[b, h] + vcols[:, b:b + 1] * kh[b:b + 1, :]
        n_ref[:, h, :] = inter * n0 + wts * kh
        m_cols.append(m)
    m_ref[...] = jnp.concatenate(m_cols, axis=1)


def _mlstm_step(u, conv_state, bv, gc, conv_w, conv_b, c0, n0, m0, nb):
    b = u.shape[0]
    assert b % nb == 0
    row = lambda w: pl.BlockSpec((nb, w), lambda i: (i, 0))
    outs = [jax.ShapeDtypeStruct((b, W_BV), F32),
            jax.ShapeDtypeStruct((b, H_B, DV, DQK), F32),
            jax.ShapeDtypeStruct((b, H_B, DQK), F32),
            jax.ShapeDtypeStruct((b, H_B), F32),
            jax.ShapeDtypeStruct((b, (CONV_W - 1) * W_U), F32)]
    cspec = pl.BlockSpec((nb, H_B, DV, DQK), lambda i: (i, 0, 0, 0))
    nspec = pl.BlockSpec((nb, H_B, DQK), lambda i: (i, 0, 0))
    return pl.pallas_call(
        functools.partial(_mlstm_step_kernel, nb=nb),
        grid=(b // nb,),
        in_specs=[row(W_U), row((CONV_W - 1) * W_U), row(W_BV), row(LANES),
                  pl.BlockSpec((CONV_W, W_U), lambda i: (0, 0)),
                  pl.BlockSpec((1, W_U), lambda i: (0, 0)),
                  cspec, nspec, row(H_B)],
        out_specs=[row(W_BV), cspec, nspec, row(H_B), row((CONV_W - 1) * W_U)],
        out_shape=outs,
        compiler_params=_params(1),
        name="mlstm_step",
    )(u, conv_state, bv, gc, conv_w, conv_b, c0, n0, m0)


def _rope_tables(pos):
    half = D_HEAD // 2
    inv = 1.0 / (ROPE_THETA ** (jnp.arange(half, dtype=F32) / half))
    ang = pos.astype(F32)[:, None] * inv[None, :]
    cos = jnp.cos(ang)
    sin = jnp.sin(ang)
    return jnp.tile(cos, (1, 4)), jnp.tile(jnp.concatenate([-sin, sin], axis=1), (1, 2))


def kernel(x_prompt, x_sample, cache_k, cache_v, page_table, state_C, state_n, state_m, state_conv, meta_tokens, ffn1_norm, ffn1_w_in, ffn1_w_out, mix_norm, w_in, b_if, b_gate, q_norm, k_norm, lam_q1, lam_k1, lam_q2, lam_k2, sub_norm, conv_w, conv_b, mh_norm, p_a, p_b, w_o, ffn2_norm, ffn2_w_in, ffn2_w_out):
    bp, s, d = x_prompt.shape
    bs = x_sample.shape[0]
    assert x_sample.shape[1] == 1 and w_in.shape[0] == 1
    n_pages = page_table.shape[1]
    page = cache_k.shape[2]

    w = w_in[0]
    o = 0
    cols = {}
    for name, width in (("q", W_A), ("k", W_A), ("v", W_A), ("u", W_U), ("bv", W_BV), ("bo", W_BV),
                        ("if", 2 * H_B), ("g", 2 * d)):
        cols[name] = w[:, o:o + width]
        o += width
    gidx = jnp.arange(2 * LANES) // D_HEAD
    wp = dict(
        wqk=jnp.concatenate([cols["q"], cols["k"]], axis=1).astype(BF16),
        wv=cols["v"].astype(BF16), wu=cols["u"].astype(BF16), wbv=cols["bv"].astype(BF16),
        wif=jnp.pad(cols["if"], ((0, 0), (0, LANES - 2 * H_B))).astype(BF16),
        bif=jnp.pad(b_if[0], (0, LANES - 2 * H_B)).reshape(1, LANES),
        qn=jnp.tile(q_norm[0], LANES // D_HEAD).reshape(1, LANES),
        kn=jnp.tile(k_norm[0], LANES // D_HEAD).reshape(1, LANES),
        gmat=jnp.where(gidx[:, None] == gidx[None, :], 1.0 / D_HEAD, 0.0).astype(BF16),
        wg=cols["g"].astype(BF16), bg=b_gate[0].reshape(1, 2 * d), wbo=cols["bo"].astype(BF16),
        mh=mh_norm[0].reshape(1, W_BV), pa=p_a[0].astype(BF16), pb=p_b[0].astype(BF16),
        wo=w_o[0].astype(BF16),
    )
    f1 = (ffn1_norm[0].reshape(1, d), ffn1_w_in[0].astype(BF16), ffn1_w_out[0].astype(BF16))
    f2 = (ffn2_norm[0].reshape(1, d), ffn2_w_in[0].astype(BF16), ffn2_w_out[0].astype(BF16))
    mixn = mix_norm[0].reshape(1, d)
    lam = jnp.stack([lam_q1[0], lam_k1[0], lam_q2[0], lam_k2[0]])
    subn = sub_norm[0].reshape(1, 2 * D_HEAD)
    cw, cb = conv_w[0], conv_b[0].reshape(1, W_U)

    tm = 512 if (bp * s) % 512 == 0 else 256
    xr = x_prompt.reshape(bp * s, d)
    n_small = -(-(bs + N_META) // 256) * 256
    xs = jnp.concatenate([x_sample.reshape(bs, d), meta_tokens.astype(F32),
                          jnp.zeros((n_small - bs - N_META, d), F32)], axis=0)
    past = n_pages * page
    pos_small = jnp.concatenate([jnp.full((bs,), past, jnp.int32), jnp.arange(N_META, dtype=jnp.int32),
                                 jnp.zeros((n_small - bs - N_META,), jnp.int32)])
    cos_r, sin_r = _rope_tables(N_META + jnp.arange(s))
    cos_s, sin_s = _rope_tables(pos_small)

    x1s = _ffn(xs, *f1, n_small)
    q_s, kt_s, _, v4_s, _, u_s, bv_s, gc_s, gr_s = _proj(x1s, mixn, wp, cos_s, sin_s, n_small, 1)
    ssl = slice(0, bs)
    msl = slice(bs, bs + N_META)
    k_new = kt_s[0, :, ssl].T.reshape(bs, 2 * H_A, D_HEAD)
    v_new = v4_s[:bs * H_A].reshape(bs, H_A, 2 * D_HEAD)

    x1r = _ffn(xr, *f1, tm)
    q_r, kt_r, kb_r, v4_r, vb_r, u_r, bv_r, gc_r, gr_r = _proj(x1r, mixn, wp, cos_r, sin_r, tm, s // tm)

    kt_meta = kt_s[0, :, msl]
    v4_meta = v4_s[bs * H_A:(bs + N_META) * H_A]
    k_meta, v_meta = kt_meta.T, v4_meta.reshape(N_META, W_A)
    pad_meta = lambda a: jnp.pad(a, ((0, LANES - N_META), (0, 0))).astype(BF16)
    _, c_m, n_m, m_m, t_m = _mlstm(
        u_s[msl][None], bv_s[msl][None], gc_s[msl][None], gr_s[:, msl][None], cw, cb,
        jnp.zeros((1, H_B, DV, LANES), F32), jnp.zeros((1, H_B, LANES), F32),
        jnp.full((1, H_B, LANES), NEG_INF, F32), jnp.zeros((1, 8, W_U), F32), N_META)

    tq = 1024 if s % 1024 == 0 else (512 if s % 512 == 0 else 256)
    ya_r =_prompt_attn(q_r.reshape(bp, s, W_A), kb_r.reshape(bp, s, W_A), vb_r.reshape(bp, s, W_A),
                        pad_meta(k_meta), pad_meta(v_meta), lam, subn, tq, 1)
    rep = lambda a: jnp.broadcast_to(a, (bp,) + a.shape[1:])
    hb_r, c_p, n_p, m_p, _ = _mlstm(
        u_r.reshape(bp, s, W_U), bv_r.reshape(bp, s, W_BV), gc_r.reshape(bp, s, LANES),
        jnp.swapaxes(gr_r.reshape(8, bp, s), 0, 1), cw, cb,
        rep(c_m), rep(n_m), rep(m_m), rep(t_m), CHUNK)

    pps = 32 if n_pages % 32 == 0 else n_pages
    ya_s = _decode_attn(page_table, q_s[ssl].reshape(bs, 2 * H_A, D_HEAD), k_new, v_new,
                        cache_k, cache_v, lam, subn, pps)
    hb_s, c_s, n_s, m_s, conv_s = _mlstm_step(
        u_s[ssl], state_conv[0].reshape(bs, (CONV_W - 1) * W_U), bv_s[ssl], gc_s[ssl], cw, cb,
        state_C[0], state_n[0], state_m[0], 8)

    y_r = _ffn(_mixer(x1r, ya_r.reshape(bp * s, W_A), hb_r.reshape(bp * s, W_BV), mixn, wp, tm), *f2, tm)
    y_s = _ffn(_mixer(x1s[ssl], ya_s.reshape(bs, W_A), hb_s, mixn, wp, bs), *f2, bs)

    def unpad_heads(a):
        return jnp.stack([a[:, h, ..., (h % 2) * DQK:(h % 2 + 1) * DQK] for h in range(H_B)], axis=1)

    kt_all = jnp.concatenate([jnp.broadcast_to(kt_meta[None], (bp, W_A, N_META)), kt_r], axis=2)
    k_prompt = jnp.transpose(kt_all.reshape(bp, 2 * H_A, D_HEAD, N_META + s), (0, 3, 1, 2))[None]
    v4_all = jnp.concatenate([jnp.broadcast_to(v4_meta[None], (bp, N_META * H_A, 2 * D_HEAD)),
                              v4_r.reshape(bp, s * H_A, 2 * D_HEAD)], axis=1)
    v_prompt = v4_all.reshape(1, bp, N_META + s, H_A, 2 * D_HEAD)
    return (y_r.reshape(bp, s, d), y_s.reshape(bs, 1, d), k_prompt, v_prompt,
            k_new.reshape(1, bs, 1, 2 * H_A, D_HEAD), v_new.reshape(1, bs, 1, H_A, 2 * D_HEAD),
            unpad_heads(c_p)[None], unpad_heads(n_p[:, :, None, :])[:, :, 0][None], m_p[:, :, 0][None],
            u_r.reshape(bp, s, W_U)[:, s - (CONV_W - 1):][None],
            c_s[None], n_s[None], m_s[None], conv_s.reshape(1, bs, CONV_W - 1, W_U))
```

```python
import functools

import jax
import jax.numpy as jnp
from jax import lax
from jax.experimental import pallas as pl
from jax.experimental.pallas import tpu as pltpu

F32 = jnp.float32
BF16 = jnp.bfloat16

N_META = 16
H_A = 4
D_HEAD = 64
H_B = 4
DQK = 64
DV = 128
CONV_W = 4
CHUNK = 128
ROPE_THETA = 10000.0
EPS = 1e-6
ATTN_SCALE = D_HEAD ** -0.5
Q_SCALE = ATTN_SCALE * 1.4426950408889634
LAM_INIT = 0.2
LANES = 128
W_A = 2 * H_A * D_HEAD
W_U = 2 * H_B * DQK
W_BV = H_B * DV
VMEM_LIMIT = 56 * 1024 * 1024
NEG_INF = float("-inf")


def _dot(a, b):
    return jnp.dot(a, b, preferred_element_type=F32)


def _nt(a, b):
    return lax.dot_general(a, b, (((1,), (1,)), ((), ())), preferred_element_type=F32)


def _bdot(a, b, ca, cb):
    return lax.dot_general(a, b, (((ca,), (cb,)), ((0,), (0,))), preferred_element_type=F32)


def _rms(x):
    return x * lax.rsqrt(jnp.mean(x * x, axis=-1, keepdims=True) + EPS)


def _sigmoid(x):
    return 1.0 / (1.0 + jnp.exp(-x))


def _log_sigmoid(x):
    return jnp.minimum(x, 0.0) - jnp.log1p(jnp.exp(-jnp.abs(x)))


def _split3(x):
    x1 = x.astype(BF16)
    r = x - x1.astype(F32)
    x2 = r.astype(BF16)
    x3 = (r - x2.astype(F32)).astype(BF16)
    return x1, x2, x3


def _params(n_axes):
    return pltpu.CompilerParams(dimension_semantics=("arbitrary",) * n_axes,
                                vmem_limit_bytes=VMEM_LIMIT)


def _const_spec(shape):
    nd = len(shape)
    return pl.BlockSpec(shape, lambda *_: (0,) * nd, pipeline_mode=pl.Buffered(1))


def _ffn_kernel(x_ref, g_ref, win_ref, wout_ref, o_ref, *, d_ff, ck):
    x = x_ref[...]
    h = (_rms(x) * g_ref[...]).astype(BF16)
    acc = jnp.zeros(x.shape, F32)
    for c in range(d_ff // ck):
        g = _dot(h, win_ref[:, c * ck:(c + 1) * ck])
        u = _dot(h, win_ref[:, d_ff + c * ck:d_ff + (c + 1) * ck])
        a = (g * _sigmoid(g) * u).astype(BF16)
        acc = acc + _dot(a, wout_ref[c * ck:(c + 1) * ck, :])
    o_ref[...] = x + 0.5 * acc


def _ffn(x, norm_g, w_in, w_out, tm):
    m, d = x.shape
    d_ff = w_out.shape[0]
    ck = 256
    assert m % tm == 0 and d_ff % ck == 0
    return pl.pallas_call(
        functools.partial(_ffn_kernel, d_ff=d_ff, ck=ck),
        grid=(m // tm,),
        in_specs=[pl.BlockSpec((tm, d), lambda i: (i, 0)),
                  _const_spec((1, d)),
                  _const_spec((d, 2 * d_ff)),
                  _const_spec((d_ff, d))],
        out_specs=pl.BlockSpec((tm, d), lambda i: (i, 0)),
        out_shape=jax.ShapeDtypeStruct((m, d), F32),
        compiler_params=_params(1),
        name="ffn",
    )(x, norm_g, w_in, w_out)


def _proj_kernel(x_ref, g_ref, wqk_ref, wv_ref, wu_ref, wbv_ref, wif_ref, bif_ref,
                 qn_ref, kn_ref, cos_ref, sin_ref, gmat_ref,
                 q_ref, kt_ref, kb_ref, v4_ref, vb_ref, u_ref, bv_ref, gc_ref, gr_ref):
    x = x_ref[...]
    tm = x.shape[0]
    h = (_rms(x) * g_ref[...]).astype(BF16)
    cos = cos_ref[...]
    sin = sin_ref[...]
    gmat = gmat_ref[...]
    lane = lax.broadcasted_iota(jnp.int32, (tm, LANES), 1)
    first_half = (lane % D_HEAD) < (D_HEAD // 2)

    def norm_rope(z2, nw):
        ms = _dot((z2 * z2).astype(BF16), gmat)
        y2 = z2 * lax.rsqrt(ms + EPS)
        outs = []
        for t in range(2):
            y = y2[:, t * LANES:(t + 1) * LANES] * nw
            rot = jnp.where(first_half, pltpu.roll(y, LANES - D_HEAD // 2, 1), pltpu.roll(y, D_HEAD // 2, 1))
            outs.append(y * cos + rot * sin)
        return outs

    qn = qn_ref[...]
    kn = kn_ref[...]
    zqk = _dot(h, wqk_ref[...])
    for pr in range(W_A // (2 * LANES)):
        c0 = pr * 2 * LANES
        for t, qq in enumerate(norm_rope(zqk[:, c0:c0 + 2 * LANES], qn)):
            q_ref[:, c0 + t * LANES:c0 + (t + 1) * LANES] = (qq * Q_SCALE).astype(BF16)
        for t, kk in enumerate(norm_rope(zqk[:, W_A + c0:W_A + c0 + 2 * LANES], kn)):
            kt_ref[c0 + t * LANES:c0 + (t + 1) * LANES, :] = kk.T
            kb_ref[:, c0 + t * LANES:c0 + (t + 1) * LANES] = kk.astype(BF16)
    v = _dot(h, wv_ref[...])
    for hd in range(H_A):
        v4_ref[0, pl.ds(hd, tm, stride=H_A), :] = v[:, hd * 2 * D_HEAD:(hd + 1) * 2 * D_HEAD]
    vb_ref[...] = v.astype(BF16)
    u_ref[...] = _dot(h, wu_ref[...])
    bv_ref[...] = _dot(h, wbv_ref[...])
    z = _dot(h, wif_ref[...]) + bif_ref[...]
    gc = jnp.where(lane < H_B, z, _log_sigmoid(z))
    gc_ref[...] = gc
    gr_ref[...] = gc.T[:8, :]


def _proj(x, mix_norm, wp, cos_t, sin_t, tm, pos_blocks, lead=0):
    m, d = x.shape
    seq = pos_blocks * tm
    assert m % seq == 0
    row = lambda w: pl.BlockSpec((tm, w), lambda i: (i, 0))
    tab = pl.BlockSpec((tm, LANES), lambda i: (i % pos_blocks, 0))
    outs = [
        jax.ShapeDtypeStruct((m, W_A), BF16),
        jax.ShapeDtypeStruct((m // seq, W_A, seq), F32),
        jax.ShapeDtypeStruct((m, W_A), BF16),
        jax.ShapeDtypeStruct((m // seq, (lead + seq) * H_A, 2 * D_HEAD), F32),
        jax.ShapeDtypeStruct((m, W_A), BF16),
        jax.ShapeDtypeStruct((m, W_U), F32),
        jax.ShapeDtypeStruct((m, W_BV), F32),
        jax.ShapeDtypeStruct((m, LANES), F32),
        jax.ShapeDtypeStruct((8, m), F32),
    ]
    return pl.pallas_call(
        _proj_kernel,
        grid=(m // tm,),
        in_specs=[row(d), _const_spec((1, d)),
                  _const_spec((d, 2 * W_A)), _const_spec((d, W_A)), _const_spec((d, W_U)),
                  _const_spec((d, W_BV)), _const_spec((d, LANES)), _const_spec((1, LANES)),
                  _const_spec((1, LANES)), _const_spec((1, LANES)), tab, tab,
                  _const_spec((2 * LANES, 2 * LANES))],
        out_specs=[row(W_A),
                   pl.BlockSpec((None, W_A, tm), lambda i: (i // pos_blocks, 0, i % pos_blocks)),
                   row(W_A),
                   pl.BlockSpec((pl.Element(1), pl.Element(tm * H_A), pl.Element(2 * D_HEAD)),
                                lambda i: (i // pos_blocks,
                                           pl.multiple_of((lead + (i % pos_blocks) * tm) * H_A, 8), 0)),
                   row(W_A), row(W_U), row(W_BV),
                   row(LANES), pl.BlockSpec((8, tm), lambda i: (0, i))],
        out_shape=outs,
        compiler_params=_params(1),
        name="proj",
    )(x, mix_norm, wp["wqk"], wp["wv"], wp["wu"], wp["wbv"], wp["wif"], wp["bif"],
      wp["qn"], wp["kn"], cos_t, sin_t, wp["gmat"])


def _lambda(lam_ref):
    a = jnp.sum(lam_ref[0:1, :] * lam_ref[1:2, :], axis=-1, keepdims=True)
    b = jnp.sum(lam_ref[2:3, :] * lam_ref[3:4, :], axis=-1, keepdims=True)
    return jnp.exp(a) - jnp.exp(b) + LAM_INIT


def _attn_kernel(q_ref, k_ref, v_ref, km_ref, vm_ref, lam_ref, sn_ref, o_ref,
                 qq_s, s0_s, s1_s, m_s, acc_s, *, tq, hg):
    i = pl.program_id(2)
    lane = lax.broadcasted_iota(jnp.int32, (tq, LANES), 1)
    heads = lambda a: [a[:, g * LANES:(g + 1) * LANES] for g in range(hg)]
    for g, q in enumerate(heads(q_ref[0])):
        zero = jnp.zeros_like(q)
        qq_s[g, 0:tq, :] = jnp.where(lane < D_HEAD, q, zero)
        qq_s[g, tq:2 * tq, :] = jnp.where(lane >= D_HEAD, q, zero)

    def with_ones(v):
        return jnp.stack([jnp.concatenate([vh, jnp.ones_like(vh)], axis=1) for vh in heads(v)])

    def scores(j, s_ref):
        off = pl.multiple_of(j * tq, tq)
        s_ref[...] = _bdot(qq_s[...], jnp.stack(heads(k_ref[0, pl.ds(off, tq), :])), 2, 2)

    def consume(j, s_ref, masked):
        off = pl.multiple_of(j * tq, tq)
        s = s_ref[...]
        if masked:
            row = lax.broadcasted_iota(jnp.int32, s.shape, 1)
            colk = lax.broadcasted_iota(jnp.int32, s.shape, 2)
            qidx = jnp.where(row >= tq, row - tq, row)
            s = jnp.where(colk <= qidx, s, NEG_INF)
        m = m_s[...]
        m_new = jnp.maximum(m, jnp.max(s, axis=2, keepdims=True))
        p = jnp.exp2(s - jnp.concatenate([m_new] * (tq // LANES), axis=2)).astype(BF16)
        alpha = jnp.exp2(m - m_new)
        acc_s[...] = (jnp.concatenate([alpha, alpha], axis=2) * acc_s[...]
                      + _bdot(p, with_ones(v_ref[0, pl.ds(off, tq), :]), 2, 1))
        m_s[...] = m_new

    s = _bdot(qq_s[...], jnp.stack(heads(km_ref[...])), 2, 2)
    col = lax.broadcasted_iota(jnp.int32, s.shape, 2)
    s = jnp.where(col < N_META, s, NEG_INF)
    m = jnp.max(s, axis=2, keepdims=True)
    m_s[...] = jnp.broadcast_to(m, (hg, 2 * tq, LANES))
    acc_s[...] = _bdot(jnp.exp2(s - m).astype(BF16), with_ones(vm_ref[...]), 2, 1)

    scores(0, s0_s)

    def pair(t, _):
        scores(2 * t + 1, s1_s)
        consume(2 * t, s0_s, False)
        scores(2 * t + 2, s0_s)
        consume(2 * t + 1, s1_s, False)
        return 0

    lax.fori_loop(0, i // 2, pair, 0)

    @pl.when(i % 2 == 1)
    def _():
        scores(i, s1_s)
        consume(i - 1, s0_s, False)
        consume(i, s1_s, True)

    @pl.when(i % 2 == 0)
    def _():
        consume(i, s0_s, True)

    acc = acc_s[...]
    o = acc[:, :, :LANES] / acc[:, :, LANES:]
    oa = o[:, :tq] - _lambda(lam_ref) * o[:, tq:]
    y = (_rms(oa) * sn_ref[...] * (1.0 - LAM_INIT)).astype(BF16)
    for g in range(hg):
        o_ref[0, :, g * LANES:(g + 1) * LANES] = y[g]


def _prompt_attn(q, kb, vb, kmeta, vmeta, lam, sub_norm, tq, hg):
    b, s, _ = q.shape
    assert s % tq == 0 and H_A % hg == 0
    w = hg * LANES
    return pl.pallas_call(
        functools.partial(_attn_kernel, tq=tq, hg=hg),
        grid=(b, H_A // hg, s // tq),
        in_specs=[pl.BlockSpec((1, tq, w), lambda b, h, i: (b, i, h)),
                  pl.BlockSpec((1, s, w), lambda b, h, i: (b, 0, h)),
                  pl.BlockSpec((1, s, w), lambda b, h, i: (b, 0, h)),
                  pl.BlockSpec((LANES, w), lambda b, h, i: (0, h)),
                  pl.BlockSpec((LANES, w), lambda b, h, i: (0, h)),
                  pl.BlockSpec((4, D_HEAD), lambda b, h, i: (0, 0)),
                  pl.BlockSpec((1, LANES), lambda b, h, i: (0, 0))],
        out_specs=pl.BlockSpec((1, tq, w), lambda b, h, i: (b, i, h)),
        out_shape=jax.ShapeDtypeStruct((b, s, W_A), BF16),
        scratch_shapes=[pltpu.VMEM((hg, 2 * tq, LANES), BF16),
                        pltpu.VMEM((hg, 2 * tq, tq), F32), pltpu.VMEM((hg, 2 * tq, tq), F32),
                        pltpu.VMEM((hg, 2 * tq, LANES), F32), pltpu.VMEM((hg, 2 * tq, 2 * LANES), F32)],
        compiler_params=_params(3),
        name="prompt_attn",
    )(q, kb, vb, kmeta, vmeta, lam, sub_norm)


def _mlstm_kernel(u_ref, bv_ref, gc_ref, gr_ref, cw_ref, cb_ref, c0_ref, n0_ref, m0_ref, t0_ref,
                  hb_ref, c_ref, n_ref, m_ref, t_ref, ue_ref, *, el, nb):
    c = pl.program_id(0)

    @pl.when(c == 0)
    def _():
        c_ref[...] = c0_ref[...]
        n_ref[...] = n0_ref[...]
        m_ref[...] = m0_ref[...]
        t_ref[...] = t0_ref[...]

    ri = lax.broadcasted_iota(jnp.int32, (el, el), 0)
    ci = lax.broadcasted_iota(jnp.int32, (el, el), 1)
    causal = ci <= ri
    tri = jnp.where(causal, 1.0, 0.0).astype(BF16)
    tri_t = jnp.where(ri <= ci, 1.0, 0.0).astype(BF16)
    lane = lax.broadcasted_iota(jnp.int32, (el, LANES), 1)

    qk, gc, gr, fh_cols, fh_rows = [], [], [], [], []
    for bi in range(nb):
        ue_ref[bi, 0:8, :] = t_ref[bi]
        ue_ref[bi, 8:8 + el, :] = u_ref[bi]
        conv = cb_ref[...] + ue_ref[bi, 5:5 + el, :] * cw_ref[0:1, :]
        for j in range(1, CONV_W):
            conv = conv + ue_ref[bi, 5 + j:5 + j + el, :] * cw_ref[j:j + 1, :]
        t_ref[bi] = ue_ref[bi, el:el + 8, :]
        qk.append(conv * _sigmoid(conv))
        gc.append(gc_ref[bi])
        gr.append(gr_ref[bi])
        g1, g2, g3 = _split3(gc[bi])
        fh_cols.append(_dot(tri, g1) + _dot(tri, g2) + _dot(tri, g3))
        r1, r2, r3 = _split3(gr[bi])
        fh_rows.append(_dot(r1, tri_t) + _dot(r2, tri_t) + _dot(r3, tri_t))

    groups = [(bi, h) for bi in range(nb) for h in range(H_B)]
    mine = [(lane >= DQK) if (h % 2) else (lane < DQK) for h in range(H_B)]
    qp = lambda bi, h: qk[bi][:, (h // 2) * LANES:(h // 2 + 1) * LANES]
    kp = lambda bi, h: qk[bi][:, H_B * DQK + (h // 2) * LANES:H_B * DQK + (h // 2 + 1) * LANES] * (DQK ** -0.5)
    qm = jnp.stack([jnp.where(mine[h], qp(bi, h), 0.0) for bi, h in groups])
    km = jnp.stack([jnp.where(mine[h], kp(bi, h), 0.0) for bi, h in groups])
    kpb = jnp.stack([kp(bi, h) for bi, h in groups]).astype(BF16)
    v3 = jnp.stack([bv_ref[bi, :, h * DV:(h + 1) * DV] for bi, h in groups])
    rep = lambda col: jnp.broadcast_to(col, (el, LANES))
    fhc = jnp.stack([rep(fh_cols[bi][:, H_B + h:H_B + h + 1]) for bi, h in groups])
    igc = jnp.stack([rep(gc[bi][:, h:h + 1]) for bi, h in groups])
    fhr = jnp.stack([fh_rows[bi][H_B + h:H_B + h + 1, :] for bi, h in groups])
    igr = jnp.stack([gr[bi][h:h + 1, :] for bi, h in groups])
    m0 = jnp.stack([m_ref[bi, h:h + 1, :] for bi, h in groups])
    n0 = jnp.stack([n_ref[bi, h:h + 1, :] for bi, h in groups])
    c0 = c_ref[...].reshape(nb * H_B, DV, LANES)

    dmat = jnp.where(causal[None], fhc[:, :, :el] - fhr + igr, NEG_INF)
    b_in = fhc + m0
    m = jnp.maximum(b_in, jnp.max(dmat, axis=2, keepdims=True))
    wts = jnp.exp(dmat - m[:, :, :el])
    inter = jnp.exp(b_in - m)
    qmb = qm.astype(BF16)
    s = (_bdot(qmb, kpb, 2, 2) * wts).astype(BF16)
    sv = _bdot(s, jnp.concatenate([v3, jnp.ones_like(v3)], axis=2).astype(BF16), 2, 1)
    qc = _bdot(qmb, jnp.concatenate([c0, jnp.broadcast_to(n0, c0.shape)], axis=1).astype(BF16), 2, 2)
    num = sv[:, :, :DV] + inter * qc[:, :, :DV]
    den = sv[:, :, DV:] + inter * qc[:, :, DV:]
    hn = _rms(num / jnp.maximum(jnp.abs(den), jnp.exp(-m)))

    m_new = m[:, el - 1:el, :]
    fl = fhc[:, el - 1:el, :]
    wl = jnp.exp(fl - fhc + igc - m_new)
    decay = jnp.exp(fl + m0 - m_new)
    c_new = decay * c0 + _bdot((v3 * wl).astype(BF16), km.astype(BF16), 1, 1)
    n_new = decay * n0 + jnp.sum(wl * km, axis=1, keepdims=True)
    c_ref[...] = c_new.reshape(nb, H_B, DV, LANES)
    for g, (bi, h) in enumerate(groups):
        hb_ref[bi, :, h * DV:(h + 1) * DV] = hn[g]
        n_ref[bi, h:h + 1, :] = n_new[g]
        m_ref[bi, h:h + 1, :] = m_new[g]


def _mlstm(u, bv, gc, gr, conv_w, conv_b, c0, n0, m0, t0, el):
    b, s, _ = u.shape
    nc = s // el
    assert s % el == 0 and gr.shape == (b, 8, s)
    st = lambda shape: pl.BlockSpec((b,) + shape, lambda c: (0,) * (len(shape) + 1))
    seq = lambda w: pl.BlockSpec((b, el, w), lambda c: (0, c, 0))
    outs = [jax.ShapeDtypeStruct((b, s, W_BV), F32),
            jax.ShapeDtypeStruct((b, H_B, DV, LANES), F32),
            jax.ShapeDtypeStruct((b, H_B, LANES), F32),
            jax.ShapeDtypeStruct((b, H_B, LANES), F32),
            jax.ShapeDtypeStruct((b, 8, W_U), F32)]
    return pl.pallas_call(
        functools.partial(_mlstm_kernel, el=el, nb=b),
        grid=(nc,),
        in_specs=[seq(W_U), seq(W_BV), seq(LANES),
                  pl.BlockSpec((b, 8, el), lambda c: (0, 0, c)),
                  pl.BlockSpec((CONV_W, W_U), lambda c: (0, 0)),
                  pl.BlockSpec((1, W_U), lambda c: (0, 0)),
                  st((H_B, DV, LANES)), st((H_B, LANES)), st((H_B, LANES)), st((8, W_U))],
        out_specs=[seq(W_BV), st((H_B, DV, LANES)), st((H_B, LANES)), st((H_B, LANES)), st((8, W_U))],
        out_shape=outs,
        scratch_shapes=[pltpu.VMEM((b, el + 8, W_U), F32)],
        compiler_params=_params(1),
        name="mlstm_scan",
    )(u, bv, gc, gr, conv_w, conv_b, c0, n0, m0, t0)


def _mixer_kernel(x_ref, ya_ref, hb_ref, g_ref, wg_ref, bg_ref, wbo_ref, mh_ref, pa_ref, pb_ref, wo_ref, o_ref):
    x = x_ref[...]
    d = x.shape[1]
    h = (_rms(x) * g_ref[...]).astype(BF16)
    bo = _dot(h, wbo_ref[...])
    yb = (hb_ref[...] * mh_ref[...] * _sigmoid(bo)).astype(BF16)
    pa = _dot(ya_ref[...], pa_ref[...])
    pb = _dot(yb, pb_ref[...])
    ga = _sigmoid(_dot(h, wg_ref[:, :d]) + bg_ref[:, :d])
    gb = _sigmoid(_dot(h, wg_ref[:, d:]) + bg_ref[:, d:])
    merged = (ga * pa + gb * pb).astype(BF16)
    o_ref[...] = x + _dot(merged, wo_ref[...])


def _mixer(x, ya, hbn, mix_norm, wp, tm):
    m, d = x.shape
    assert m % tm == 0
    row = lambda w: pl.BlockSpec((tm, w), lambda i: (i, 0))
    return pl.pallas_call(
        _mixer_kernel,
        grid=(m // tm,),
        in_specs=[row(d), row(W_A), row(W_BV), _const_spec((1, d)),
                  _const_spec((d, 2 * d)), _const_spec((1, 2 * d)), _const_spec((d, W_BV)),
                  _const_spec((1, W_BV)), _const_spec((W_A, d)), _const_spec((W_BV, d)),
                  _const_spec((d, d))],
        out_specs=row(d),
        out_shape=jax.ShapeDtypeStruct((m, d), F32),
        compiler_params=_params(1),
        name="mixer_out",
    )(x, ya, hbn, mix_norm, wp["wg"], wp["bg"], wp["wbo"], wp["mh"], wp["pa"], wp["pb"], wp["wo"])


def _decode_kernel(pt_ref, qrow_ref, q_ref, kn_ref, vn_ref, lam_ref, sn_ref, *refs, pps):
    k_refs = refs[:pps]
    v_refs = refs[pps:2 * pps]
    o_ref, m_s, l_s, acc_s = refs[2 * pps:]
    g = pl.program_id(1)
    ng = pl.num_programs(1)
    nsub = 2 * H_A
    page = k_refs[0].shape[1]

    @pl.when(g == 0)
    def _():
        m_s[...] = jnp.full(m_s.shape, NEG_INF, F32)
        l_s[...] = jnp.zeros(l_s.shape, F32)
        acc_s[...] = jnp.zeros(acc_s.shape, F32)

    row = lax.broadcasted_iota(jnp.int32, (nsub, W_A), 0)
    lane = lax.broadcasted_iota(jnp.int32, (nsub, W_A), 1)
    qbd = jnp.where(lane // D_HEAD == row, jnp.broadcast_to(qrow_ref[0].astype(F32), (nsub, W_A)), 0.0).astype(BF16)

    s = jnp.concatenate([_dot(qbd, k_refs[i][...].astype(BF16)) for i in range(pps)], axis=1)
    m = m_s[...]
    m_new = jnp.maximum(m, jnp.max(s, axis=1, keepdims=True))
    alpha = jnp.exp2(m - m_new)
    p = jnp.exp2(s - m_new)
    l_s[...] = alpha * l_s[...] + jnp.sum(p, axis=1, keepdims=True)
    m_s[...] = m_new
    pb = p.astype(BF16)
    for h in range(H_A):
        pv = _dot(pb[:, 0:page], v_refs[0][pl.ds(h, page, stride=H_A), :].astype(BF16))
        for i in range(1, pps):
            pv = pv + _dot(pb[:, i * page:(i + 1) * page], v_refs[i][pl.ds(h, page, stride=H_A), :].astype(BF16))
        acc_s[h] = alpha * acc_s[h] + pv

    @pl.when(g == ng - 1)
    def _():
        q8 = q_ref[0].astype(F32)
        s_self = jnp.sum(q8 * kn_ref[0].astype(BF16).astype(F32), axis=1, keepdims=True)
        m = m_s[...]
        m_new = jnp.maximum(m, s_self)
        alpha = jnp.exp2(m - m_new)
        p_self = jnp.exp2(s_self - m_new)
        l = alpha * l_s[...] + p_self
        lam = _lambda(lam_ref)
        vn = vn_ref[0].astype(BF16).astype(F32)
        outs = []
        for h in range(H_A):
            o = (alpha * acc_s[h] + p_self * vn[h:h + 1, :]) / l
            oa = o[2 * h:2 * h + 1, :] - lam * o[2 * h + 1:2 * h + 2, :]
            outs.append(_rms(oa) * sn_ref[...] * (1.0 - LAM_INIT))
        o_ref[0] = jnp.concatenate(outs, axis=1).astype(BF16)


def _decode_attn(page_table, q, k_new, v_new, cache_k, cache_v, lam, sub_norm, pps):
    b, n_pages = page_table.shape
    n_pool, page = cache_k.shape[1], cache_k.shape[2]
    nsub = 2 * H_A
    assert n_pages % pps == 0 and cache_k.shape[0] == 1
    pt = page_table.reshape(-1)
    ck = jnp.transpose(cache_k[0], (0, 2, 3, 1)).reshape(n_pool, nsub * D_HEAD, page)
    cv = cache_v[0].reshape(n_pool, page * H_A, 2 * D_HEAD)

    kspec = lambda i: pl.BlockSpec((None, nsub * D_HEAD, page),
                                   lambda b, g, pt: (pt[b * n_pages + g * pps + i], 0, 0))
    vspec = lambda i: pl.BlockSpec((None, page * H_A, 2 * D_HEAD),
                                   lambda b, g, pt: (pt[b * n_pages + g * pps + i], 0, 0))
    tok = lambda heads, width: pl.BlockSpec((1, heads, width), lambda b, g, pt: (b, 0, 0))
    grid_spec = pltpu.PrefetchScalarGridSpec(
        num_scalar_prefetch=1,
        grid=(b, n_pages // pps),
        in_specs=[tok(1, W_A), tok(nsub, D_HEAD), tok(nsub, D_HEAD), tok(H_A, 2 * D_HEAD),
                  pl.BlockSpec((4, D_HEAD), lambda b, g, pt: (0, 0)),
                  pl.BlockSpec((1, LANES), lambda b, g, pt: (0, 0))]
                 + [kspec(i) for i in range(pps)] + [vspec(i) for i in range(pps)],
        out_specs=pl.BlockSpec((1, 1, W_A), lambda b, g, pt: (b, 0, 0)),
        scratch_shapes=[pltpu.VMEM((nsub, 1), F32), pltpu.VMEM((nsub, 1), F32),
                        pltpu.VMEM((H_A, nsub, 2 * D_HEAD), F32)],
    )
    return pl.pallas_call(
        functools.partial(_decode_kernel, pps=pps),
        grid_spec=grid_spec,
        out_shape=jax.ShapeDtypeStruct((b, 1, W_A), BF16),
        compiler_params=_params(2),
        name="decode_attn",
    )(pt, q.reshape(b, 1, W_A), q, k_new, v_new, lam, sub_norm, *([ck] * pps), *([cv] * pps))


def _mlstm_step_kernel(u_ref, cs_ref, bv_ref, gc_ref, cw_ref, cb_ref, c0_ref, n0_ref, m0_ref,
                       hb_ref, c_ref, n_ref, m_ref, cso_ref, *, nb):
    u = u_ref[...]
    cs = cs_ref[...]
    conv = cb_ref[...] + u * cw_ref[CONV_W - 1:CONV_W, :]
    for j in range(CONV_W - 1):
        conv = conv + cs[:, j * W_U:(j + 1) * W_U] * cw_ref[j:j + 1, :]
    cso_ref[...] = jnp.concatenate([cs[:, W_U:], u], axis=1)
    qk = conv * _sigmoid(conv)
    gc = gc_ref[...]
    bv = bv_ref[...]
    m0_all = m0_ref[...]
    ident = jnp.where(lax.broadcasted_iota(jnp.int32, (DQK, DQK), 0) == lax.broadcasted_iota(jnp.int32, (DQK, DQK), 1),
                      1.0, 0.0).astype(BF16)
    m_cols = []
    for h in range(H_B):
        qh = qk[:, h * DQK:(h + 1) * DQK]
        kh = qk[:, H_B * DQK + h * DQK:H_B * DQK + (h + 1) * DQK] * (DQK ** -0.5)
        vh = bv[:, h * DV:(h + 1) * DV]
        ig = gc[:, h:h + 1]
        logf = gc[:, H_B + h:H_B + h + 1]
        m0 = m0_all[:, h:h + 1]
        n0 = n0_ref[:, h, :]
        b_in = logf + m0
        m = jnp.maximum(b_in, ig)
        wts = jnp.exp(ig - m)
        inter = jnp.exp(b_in - m)
        s = jnp.sum(qh * kh, axis=1, keepdims=True) * wts
        qhb = qh.astype(BF16)
        cq_rows = []
        for b in range(nb):
            cq_rows.append(_dot(qhb, c0_ref[b, h].astype(BF16))[b:b + 1, :])
        cq = jnp.concatenate(cq_rows, axis=0)
        num = s * vh + inter * cq
        den = s + inter * jnp.sum(n0 * qh, axis=1, keepdims=True)
        hout = num / jnp.maximum(jnp.abs(den), jnp.exp(-m))
        hb_ref[:, h * DV:(h + 1) * DV] = _rms(hout)
        kcols = _nt(ident, kh.astype(BF16))
        vw = vh * wts
        for b in range(nb):
            c_ref[b, h] = inter[b:b + 1, :] * c0_ref[b, h] + kcols[:, b:b + 1] * vw[b:b + 1, :]
        n_ref[:, h, :] = inter * n0 + wts * kh
        m_cols.append(m)
    m_ref[...] = jnp.concatenate(m_cols, axis=1)


def _mlstm_step(u, conv_state, bv, gc, conv_w, conv_b, c0, n0, m0, nb):
    b = u.shape[0]
    assert b % nb == 0
    row = lambda w: pl.BlockSpec((nb, w), lambda i: (i, 0))
    outs = [jax.ShapeDtypeStruct((b, W_BV), F32),
            jax.ShapeDtypeStruct((b, H_B, DQK, DV), F32),
            jax.ShapeDtypeStruct((b, H_B, DQK), F32),
            jax.ShapeDtypeStruct((b, H_B), F32),
            jax.ShapeDtypeStruct((b, (CONV_W - 1) * W_U), F32)]
    cspec = pl.BlockSpec((nb, H_B, DQK, DV), lambda i: (i, 0, 0, 0))
    nspec = pl.BlockSpec((nb, H_B, DQK), lambda i: (i, 0, 0))
    return pl.pallas_call(
        functools.partial(_mlstm_step_kernel, nb=nb),
        grid=(b // nb,),
        in_specs=[row(W_U), row((CONV_W - 1) * W_U), row(W_BV), row(LANES),
                  pl.BlockSpec((CONV_W, W_U), lambda i: (0, 0)),
                  pl.BlockSpec((1, W_U), lambda i: (0, 0)),
                  cspec, nspec, row(H_B)],
        out_specs=[row(W_BV), cspec, nspec, row(H_B), row((CONV_W - 1) * W_U)],
        out_shape=outs,
        compiler_params=_params(1),
        name="mlstm_step",
    )(u, conv_state, bv, gc, conv_w, conv_b, c0, n0, m0)


def _rope_tables(pos):
    half = D_HEAD // 2
    inv = 1.0 / (ROPE_THETA ** (jnp.arange(half, dtype=F32) / half))
    ang = pos.astype(F32)[:, None] * inv[None, :]
    cos = jnp.cos(ang)
    sin = jnp.sin(ang)
    return jnp.tile(cos, (1, 4)), jnp.tile(jnp.concatenate([-sin, sin], axis=1), (1, 2))


def kernel(x_prompt, x_sample, cache_k, cache_v, page_table, state_C, state_n, state_m, state_conv, meta_tokens, ffn1_norm, ffn1_w_in, ffn1_w_out, mix_norm, w_in, b_if, b_gate, q_norm, k_norm, lam_q1, lam_k1, lam_q2, lam_k2, sub_norm, conv_w, conv_b, mh_norm, p_a, p_b, w_o, ffn2_norm, ffn2_w_in, ffn2_w_out):
    bp, s, d = x_prompt.shape
    bs = x_sample.shape[0]
    assert x_sample.shape[1] == 1 and w_in.shape[0] == 1
    n_pages = page_table.shape[1]
    page = cache_k.shape[2]

    w = w_in[0]
    o = 0
    cols = {}
    for name, width in (("q", W_A), ("k", W_A), ("v", W_A), ("u", W_U), ("bv", W_BV), ("bo", W_BV),
                        ("if", 2 * H_B), ("g", 2 * d)):
        cols[name] = w[:, o:o + width]
        o += width
    gidx = jnp.arange(2 * LANES) // D_HEAD
    wp = dict(
        wqk=jnp.concatenate([cols["q"], cols["k"]], axis=1).astype(BF16),
        wv=cols["v"].astype(BF16), wu=cols["u"].astype(BF16), wbv=cols["bv"].astype(BF16),
        wif=jnp.pad(cols["if"], ((0, 0), (0, LANES - 2 * H_B))).astype(BF16),
        bif=jnp.pad(b_if[0], (0, LANES - 2 * H_B)).reshape(1, LANES),
        qn=jnp.tile(q_norm[0], LANES // D_HEAD).reshape(1, LANES),
        kn=jnp.tile(k_norm[0], LANES // D_HEAD).reshape(1, LANES),
        gmat=jnp.where(gidx[:, None] == gidx[None, :], 1.0 / D_HEAD, 0.0).astype(BF16),
        wg=cols["g"].astype(BF16), bg=b_gate[0].reshape(1, 2 * d), wbo=cols["bo"].astype(BF16),
        mh=mh_norm[0].reshape(1, W_BV), pa=p_a[0].astype(BF16), pb=p_b[0].astype(BF16),
        wo=w_o[0].astype(BF16),
    )
    f1 = (ffn1_norm[0].reshape(1, d), ffn1_w_in[0].astype(BF16), ffn1_w_out[0].astype(BF16))
    f2 = (ffn2_norm[0].reshape(1, d), ffn2_w_in[0].astype(BF16), ffn2_w_out[0].astype(BF16))
    mixn = mix_norm[0].reshape(1, d)
    lam = jnp.stack([lam_q1[0], lam_k1[0], lam_q2[0], lam_k2[0]])
    subn = sub_norm[0].reshape(1, 2 * D_HEAD)
    cw, cb = conv_w[0], conv_b[0].reshape(1, W_U)

    tm = 512 if (bp * s) % 512 == 0 else 256
    xr = x_prompt.reshape(bp * s, d)
    n_small = -(-(bs + N_META) // 256) * 256
    xs = jnp.concatenate([x_sample.reshape(bs, d), meta_tokens.astype(F32),
                          jnp.zeros((n_small - bs - N_META, d), F32)], axis=0)
    past = n_pages * page
    pos_small = jnp.concatenate([jnp.full((bs,), past, jnp.int32), jnp.arange(N_META, dtype=jnp.int32),
                                 jnp.zeros((n_small - bs - N_META,), jnp.int32)])
    cos_r, sin_r = _rope_tables(N_META + jnp.arange(s))
    cos_s, sin_s = _rope_tables(pos_small)

    x1s = _ffn(xs, *f1, n_small)
    q_s, kt_s, _, v4_s, _, u_s, bv_s, gc_s, gr_s = _proj(x1s, mixn, wp, cos_s, sin_s, n_small, 1)
    v4_s = v4_s[0]
    ssl = slice(0, bs)
    msl = slice(bs, bs + N_META)
    k_new = kt_s[0, :, ssl].T.reshape(bs, 2 * H_A, D_HEAD)
    v_new = v4_s[:bs * H_A].reshape(bs, H_A, 2 * D_HEAD)

    x1r = _ffn(xr, *f1, tm)
    q_r, kt_r, kb_r, v4_r, vb_r, u_r, bv_r, gc_r, gr_r = _proj(x1r, mixn, wp, cos_r, sin_r, tm, s // tm, N_META)

    kt_meta = kt_s[0, :, msl]
    v4_meta = v4_s[bs * H_A:(bs + N_META) * H_A]
    k_meta, v_meta = kt_meta.T, v4_meta.reshape(N_META, W_A)
    pad_meta = lambda a: jnp.pad(a, ((0, LANES - N_META), (0, 0))).astype(BF16)
    _, c_m, n_m, m_m, t_m = _mlstm(
        u_s[msl][None], bv_s[msl][None], gc_s[msl][None], gr_s[:, msl][None], cw, cb,
        jnp.zeros((1, H_B, DV, LANES), F32), jnp.zeros((1, H_B, LANES), F32),
        jnp.full((1, H_B, LANES), NEG_INF, F32), jnp.zeros((1, 8, W_U), F32), N_META)

    tq = 1024 if s % 1024 == 0 else (512 if s % 512 == 0 else 256)
    ya_r =_prompt_attn(q_r.reshape(bp, s, W_A), kb_r.reshape(bp, s, W_A), vb_r.reshape(bp, s, W_A),
                        pad_meta(k_meta), pad_meta(v_meta), lam, subn, tq, 1)
    rep = lambda a: jnp.broadcast_to(a, (bp,) + a.shape[1:])
    hb_r, c_p, n_p, m_p, _ = _mlstm(
        u_r.reshape(bp, s, W_U), bv_r.reshape(bp, s, W_BV), gc_r.reshape(bp, s, LANES),
        jnp.swapaxes(gr_r.reshape(8, bp, s), 0, 1), cw, cb,
        rep(c_m), rep(n_m), rep(m_m), rep(t_m), CHUNK)

    pps = 32 if n_pages % 32 == 0 else n_pages
    ya_s = _decode_attn(page_table, q_s[ssl].reshape(bs, 2 * H_A, D_HEAD), k_new, v_new,
                        cache_k, cache_v, lam, subn, pps)
    hb_s, c_s, n_s, m_s, conv_s = _mlstm_step(
        u_s[ssl], state_conv[0].reshape(bs, (CONV_W - 1) * W_U), bv_s[ssl], gc_s[ssl], cw, cb,
        jnp.swapaxes(state_C[0], 2, 3), state_n[0], state_m[0], 8)

    y_r = _ffn(_mixer(x1r, ya_r.reshape(bp * s, W_A), hb_r.reshape(bp * s, W_BV), mixn, wp, tm), *f2, tm)
    y_s = _ffn(_mixer(x1s[ssl], ya_s.reshape(bs, W_A), hb_s, mixn, wp, bs), *f2, bs)

    def unpad_heads(a):
        return jnp.stack([a[:, h, ..., (h % 2) * DQK:(h % 2 + 1) * DQK] for h in range(H_B)], axis=1)

    kt_all = jnp.concatenate([jnp.broadcast_to(kt_meta[None], (bp, W_A, N_META)), kt_r], axis=2)
    k_prompt = jnp.transpose(kt_all.reshape(bp, 2 * H_A, D_HEAD, N_META + s), (0, 3, 1, 2))[None]
    v4_all = v4_r.at[:, :N_META * H_A].set(jnp.broadcast_to(v4_meta[None], (bp, N_META * H_A, 2 * D_HEAD)))
    v_prompt = v4_all.reshape(1, bp, N_META + s, H_A, 2 * D_HEAD)
    return (y_r.reshape(bp, s, d), y_s.reshape(bs, 1, d), k_prompt, v_prompt,
            k_new.reshape(1, bs, 1, 2 * H_A, D_HEAD), v_new.reshape(1, bs, 1, H_A, 2 * D_HEAD),
            unpad_heads(c_p)[None], unpad_heads(n_p[:, :, None, :])[:, :, 0][None], m_p[:, :, 0][None],
            u_r.reshape(bp, s, W_U)[:, s - (CONV_W - 1):][None],
            jnp.swapaxes(c_s, 2, 3)[None], n_s[None], m_s[None], conv_s.reshape(1, bs, CONV_W - 1, W_U))
```

```python
import functools

import jax
import jax.numpy as jnp
from jax import lax
from jax.experimental import pallas as pl
from jax.experimental.pallas import tpu as pltpu

F32 = jnp.float32
BF16 = jnp.bfloat16

N_META = 16
H_A = 4
D_HEAD = 64
H_B = 4
DQK = 64
DV = 128
CONV_W = 4
CHUNK = 128
ROPE_THETA = 10000.0
EPS = 1e-6
ATTN_SCALE = D_HEAD ** -0.5
Q_SCALE = ATTN_SCALE * 1.4426950408889634
LAM_INIT = 0.2
LANES = 128
W_A = 2 * H_A * D_HEAD
W_U = 2 * H_B * DQK
W_BV = H_B * DV
VMEM_LIMIT = 56 * 1024 * 1024
NEG_INF = float("-inf")


def _dot(a, b):
    return jnp.dot(a, b, preferred_element_type=F32)


def _nt(a, b):
    return lax.dot_general(a, b, (((1,), (1,)), ((), ())), preferred_element_type=F32)


def _bdot(a, b, ca, cb):
    return lax.dot_general(a, b, (((ca,), (cb,)), ((0,), (0,))), preferred_element_type=F32)


def _rms(x):
    return x * lax.rsqrt(jnp.mean(x * x, axis=-1, keepdims=True) + EPS)


def _sigmoid(x):
    return 1.0 / (1.0 + jnp.exp(-x))


def _log_sigmoid(x):
    return jnp.minimum(x, 0.0) - jnp.log1p(jnp.exp(-jnp.abs(x)))


def _split3(x):
    x1 = x.astype(BF16)
    r = x - x1.astype(F32)
    x2 = r.astype(BF16)
    x3 = (r - x2.astype(F32)).astype(BF16)
    return x1, x2, x3


def _params(n_axes):
    return pltpu.CompilerParams(dimension_semantics=("arbitrary",) * n_axes,
                                vmem_limit_bytes=VMEM_LIMIT)


def _const_spec(shape):
    nd = len(shape)
    return pl.BlockSpec(shape, lambda *_: (0,) * nd, pipeline_mode=pl.Buffered(1))


def _ffn_rows(x, g_ref, win_ref, wout_ref, d_ff, ck):
    h = (_rms(x) * g_ref[...]).astype(BF16)
    acc = jnp.zeros(x.shape, F32)
    for c in range(d_ff // ck):
        g = _dot(h, win_ref[:, c * ck:(c + 1) * ck])
        u = _dot(h, win_ref[:, d_ff + c * ck:d_ff + (c + 1) * ck])
        a = (g * _sigmoid(g) * u).astype(BF16)
        acc = acc + _dot(a, wout_ref[c * ck:(c + 1) * ck, :])
    return x + 0.5 * acc


def _ffn_kernel(x_ref, g_ref, win_ref, wout_ref, o_ref, *, d_ff, ck):
    o_ref[...] = _ffn_rows(x_ref[...], g_ref, win_ref, wout_ref, d_ff, ck)


def _ffn(x, norm_g, w_in, w_out, tm):
    m, d = x.shape
    d_ff = w_out.shape[0]
    ck = 256
    assert m % tm == 0 and d_ff % ck == 0
    return pl.pallas_call(
        functools.partial(_ffn_kernel, d_ff=d_ff, ck=ck),
        grid=(m // tm,),
        in_specs=[pl.BlockSpec((tm, d), lambda i: (i, 0)),
                  _const_spec((1, d)),
                  _const_spec((d, 2 * d_ff)),
                  _const_spec((d_ff, d))],
        out_specs=pl.BlockSpec((tm, d), lambda i: (i, 0)),
        out_shape=jax.ShapeDtypeStruct((m, d), F32),
        compiler_params=_params(1),
        name="ffn",
    )(x, norm_g, w_in, w_out)


def _proj_kernel(x_ref, g_ref, wqk_ref, wv_ref, wu_ref, wbv_ref, wif_ref, bif_ref,
                 qn_ref, kn_ref, cos_ref, sin_ref, gmat_ref,
                 q_ref, kt_ref, kb_ref, v4_ref, vb_ref, u_ref, bv_ref, gc_ref, gr_ref):
    x = x_ref[...]
    tm = x.shape[0]
    h = (_rms(x) * g_ref[...]).astype(BF16)
    cos = cos_ref[...]
    sin = sin_ref[...]
    gmat = gmat_ref[...]
    lane = lax.broadcasted_iota(jnp.int32, (tm, LANES), 1)
    first_half = (lane % D_HEAD) < (D_HEAD // 2)

    def norm_rope(z2, nw):
        ms = _dot((z2 * z2).astype(BF16), gmat)
        y2 = z2 * lax.rsqrt(ms + EPS)
        outs = []
        for t in range(2):
            y = y2[:, t * LANES:(t + 1) * LANES] * nw
            rot = jnp.where(first_half, pltpu.roll(y, LANES - D_HEAD // 2, 1), pltpu.roll(y, D_HEAD // 2, 1))
            outs.append(y * cos + rot * sin)
        return outs

    qn = qn_ref[...]
    kn = kn_ref[...]
    zqk = _dot(h, wqk_ref[...])
    for pr in range(W_A // (2 * LANES)):
        c0 = pr * 2 * LANES
        for t, qq in enumerate(norm_rope(zqk[:, c0:c0 + 2 * LANES], qn)):
            q_ref[:, c0 + t * LANES:c0 + (t + 1) * LANES] = (qq * Q_SCALE).astype(BF16)
        for t, kk in enumerate(norm_rope(zqk[:, W_A + c0:W_A + c0 + 2 * LANES], kn)):
            kt_ref[c0 + t * LANES:c0 + (t + 1) * LANES, :] = kk.T
            kb_ref[:, c0 + t * LANES:c0 + (t + 1) * LANES] = kk.astype(BF16)
    v = _dot(h, wv_ref[...])
    for hd in range(H_A):
        v4_ref[0, pl.ds(hd, tm, stride=H_A), :] = v[:, hd * 2 * D_HEAD:(hd + 1) * 2 * D_HEAD]
    vb_ref[...] = v.astype(BF16)
    u_ref[...] = _dot(h, wu_ref[...])
    bv_ref[...] = _dot(h, wbv_ref[...])
    z = _dot(h, wif_ref[...]) + bif_ref[...]
    gc = jnp.where(lane < H_B, z, _log_sigmoid(z))
    gc_ref[...] = gc
    gr_ref[...] = gc.T[:8, :]


def _proj(x, mix_norm, wp, cos_t, sin_t, tm, pos_blocks, lead=0):
    m, d = x.shape
    seq = pos_blocks * tm
    assert m % seq == 0
    row = lambda w: pl.BlockSpec((tm, w), lambda i: (i, 0))
    tab = pl.BlockSpec((tm, LANES), lambda i: (i % pos_blocks, 0))
    outs = [
        jax.ShapeDtypeStruct((m, W_A), BF16),
        jax.ShapeDtypeStruct((m // seq, W_A, seq), F32),
        jax.ShapeDtypeStruct((m, W_A), BF16),
        jax.ShapeDtypeStruct((m // seq, (lead + seq) * H_A, 2 * D_HEAD), F32),
        jax.ShapeDtypeStruct((m, W_A), BF16),
        jax.ShapeDtypeStruct((m, W_U), F32),
        jax.ShapeDtypeStruct((m, W_BV), F32),
        jax.ShapeDtypeStruct((m, LANES), F32),
        jax.ShapeDtypeStruct((8, m), F32),
    ]
    return pl.pallas_call(
        _proj_kernel,
        grid=(m // tm,),
        in_specs=[row(d), _const_spec((1, d)),
                  _const_spec((d, 2 * W_A)), _const_spec((d, W_A)), _const_spec((d, W_U)),
                  _const_spec((d, W_BV)), _const_spec((d, LANES)), _const_spec((1, LANES)),
                  _const_spec((1, LANES)), _const_spec((1, LANES)), tab, tab,
                  _const_spec((2 * LANES, 2 * LANES))],
        out_specs=[row(W_A),
                   pl.BlockSpec((None, W_A, tm), lambda i: (i // pos_blocks, 0, i % pos_blocks)),
                   row(W_A),
                   pl.BlockSpec((pl.Element(1), pl.Element(tm * H_A), pl.Element(2 * D_HEAD)),
                                lambda i: (i // pos_blocks,
                                           pl.multiple_of((lead + (i % pos_blocks) * tm) * H_A, 8), 0)),
                   row(W_A), row(W_U), row(W_BV),
                   row(LANES), pl.BlockSpec((8, tm), lambda i: (0, i))],
        out_shape=outs,
        compiler_params=_params(1),
        name="proj",
    )(x, mix_norm, wp["wqk"], wp["wv"], wp["wu"], wp["wbv"], wp["wif"], wp["bif"],
      wp["qn"], wp["kn"], cos_t, sin_t, wp["gmat"])


def _lambda(lam_ref):
    a = jnp.sum(lam_ref[0:1, :] * lam_ref[1:2, :], axis=-1, keepdims=True)
    b = jnp.sum(lam_ref[2:3, :] * lam_ref[3:4, :], axis=-1, keepdims=True)
    return jnp.exp(a) - jnp.exp(b) + LAM_INIT


def _attn_kernel(q_ref, k_ref, v_ref, km_ref, vm_ref, lam_ref, sn_ref, o_ref,
                 qq_s, s0_s, s1_s, m_s, acc_s, *, tq, hg):
    i = pl.program_id(2)
    lane = lax.broadcasted_iota(jnp.int32, (tq, LANES), 1)
    heads = lambda a: [a[:, g * LANES:(g + 1) * LANES] for g in range(hg)]
    for g, q in enumerate(heads(q_ref[0])):
        zero = jnp.zeros_like(q)
        qq_s[g, 0:tq, :] = jnp.where(lane < D_HEAD, q, zero)
        qq_s[g, tq:2 * tq, :] = jnp.where(lane >= D_HEAD, q, zero)

    def with_ones(v):
        return jnp.stack([jnp.concatenate([vh, jnp.ones_like(vh)], axis=1) for vh in heads(v)])

    def scores(j, s_ref):
        off = pl.multiple_of(j * tq, tq)
        s_ref[...] = _bdot(qq_s[...], jnp.stack(heads(k_ref[0, pl.ds(off, tq), :])), 2, 2)

    def consume(j, s_ref, masked):
        off = pl.multiple_of(j * tq, tq)
        s = s_ref[...]
        if masked:
            row = lax.broadcasted_iota(jnp.int32, s.shape, 1)
            colk = lax.broadcasted_iota(jnp.int32, s.shape, 2)
            qidx = jnp.where(row >= tq, row - tq, row)
            s = jnp.where(colk <= qidx, s, NEG_INF)
        m = m_s[...]
        m_new = jnp.maximum(m, jnp.max(s, axis=2, keepdims=True))
        p = jnp.exp2(s - jnp.concatenate([m_new] * (tq // LANES), axis=2)).astype(BF16)
        alpha = jnp.exp2(m - m_new)
        acc_s[...] = (jnp.concatenate([alpha, alpha], axis=2) * acc_s[...]
                      + _bdot(p, with_ones(v_ref[0, pl.ds(off, tq), :]), 2, 1))
        m_s[...] = m_new

    s = _bdot(qq_s[...], jnp.stack(heads(km_ref[...])), 2, 2)
    col = lax.broadcasted_iota(jnp.int32, s.shape, 2)
    s = jnp.where(col < N_META, s, NEG_INF)
    m = jnp.max(s, axis=2, keepdims=True)
    m_s[...] = jnp.broadcast_to(m, (hg, 2 * tq, LANES))
    acc_s[...] = _bdot(jnp.exp2(s - m).astype(BF16), with_ones(vm_ref[...]), 2, 1)

    scores(0, s0_s)

    def pair(t, _):
        scores(2 * t + 1, s1_s)
        consume(2 * t, s0_s, False)
        scores(2 * t + 2, s0_s)
        consume(2 * t + 1, s1_s, False)
        return 0

    lax.fori_loop(0, i // 2, pair, 0)

    @pl.when(i % 2 == 1)
    def _():
        scores(i, s1_s)
        consume(i - 1, s0_s, False)
        consume(i, s1_s, True)

    @pl.when(i % 2 == 0)
    def _():
        consume(i, s0_s, True)

    acc = acc_s[...]
    o = acc[:, :, :LANES] / acc[:, :, LANES:]
    oa = o[:, :tq] - _lambda(lam_ref) * o[:, tq:]
    y = (_rms(oa) * sn_ref[...] * (1.0 - LAM_INIT)).astype(BF16)
    for g in range(hg):
        o_ref[0, :, g * LANES:(g + 1) * LANES] = y[g]


def _prompt_attn(q, kb, vb, kmeta, vmeta, lam, sub_norm, tq, hg):
    b, s, _ = q.shape
    assert s % tq == 0 and H_A % hg == 0
    w = hg * LANES
    return pl.pallas_call(
        functools.partial(_attn_kernel, tq=tq, hg=hg),
        grid=(b, H_A // hg, s // tq),
        in_specs=[pl.BlockSpec((1, tq, w), lambda b, h, i: (b, i, h)),
                  pl.BlockSpec((1, s, w), lambda b, h, i: (b, 0, h)),
                  pl.BlockSpec((1, s, w), lambda b, h, i: (b, 0, h)),
                  pl.BlockSpec((LANES, w), lambda b, h, i: (0, h)),
                  pl.BlockSpec((LANES, w), lambda b, h, i: (0, h)),
                  pl.BlockSpec((4, D_HEAD), lambda b, h, i: (0, 0)),
                  pl.BlockSpec((1, LANES), lambda b, h, i: (0, 0))],
        out_specs=pl.BlockSpec((1, tq, w), lambda b, h, i: (b, i, h)),
        out_shape=jax.ShapeDtypeStruct((b, s, W_A), BF16),
        scratch_shapes=[pltpu.VMEM((hg, 2 * tq, LANES), BF16),
                        pltpu.VMEM((hg, 2 * tq, tq), F32), pltpu.VMEM((hg, 2 * tq, tq), F32),
                        pltpu.VMEM((hg, 2 * tq, LANES), F32), pltpu.VMEM((hg, 2 * tq, 2 * LANES), F32)],
        compiler_params=_params(3),
        name="prompt_attn",
    )(q, kb, vb, kmeta, vmeta, lam, sub_norm)


def _mlstm_kernel(u_ref, bv_ref, gc_ref, gr_ref, cw_ref, cb_ref, c0_ref, n0_ref, m0_ref, t0_ref,
                  hb_ref, c_ref, n_ref, m_ref, t_ref, ue_ref, *, el, nb):
    c = pl.program_id(0)

    @pl.when(c == 0)
    def _():
        c_ref[...] = c0_ref[...]
        n_ref[...] = n0_ref[...]
        m_ref[...] = m0_ref[...]
        t_ref[...] = t0_ref[...]

    ri = lax.broadcasted_iota(jnp.int32, (el, el), 0)
    ci = lax.broadcasted_iota(jnp.int32, (el, el), 1)
    causal = ci <= ri
    tri = jnp.where(causal, 1.0, 0.0).astype(BF16)
    tri_t = jnp.where(ri <= ci, 1.0, 0.0).astype(BF16)
    lane = lax.broadcasted_iota(jnp.int32, (el, LANES), 1)

    qk, gc, gr, fh_cols, fh_rows = [], [], [], [], []
    for bi in range(nb):
        ue_ref[bi, 0:8, :] = t_ref[bi]
        ue_ref[bi, 8:8 + el, :] = u_ref[bi]
        conv = cb_ref[...] + ue_ref[bi, 5:5 + el, :] * cw_ref[0:1, :]
        for j in range(1, CONV_W):
            conv = conv + ue_ref[bi, 5 + j:5 + j + el, :] * cw_ref[j:j + 1, :]
        t_ref[bi] = ue_ref[bi, el:el + 8, :]
        qk.append(conv * _sigmoid(conv))
        gc.append(gc_ref[bi])
        gr.append(gr_ref[bi])
        g1, g2, g3 = _split3(gc[bi])
        fh_cols.append(_dot(tri, g1) + _dot(tri, g2) + _dot(tri, g3))
        r1, r2, r3 = _split3(gr[bi])
        fh_rows.append(_dot(r1, tri_t) + _dot(r2, tri_t) + _dot(r3, tri_t))

    groups = [(bi, h) for bi in range(nb) for h in range(H_B)]
    mine = [(lane >= DQK) if (h % 2) else (lane < DQK) for h in range(H_B)]
    qp = lambda bi, h: qk[bi][:, (h // 2) * LANES:(h // 2 + 1) * LANES]
    kp = lambda bi, h: qk[bi][:, H_B * DQK + (h // 2) * LANES:H_B * DQK + (h // 2 + 1) * LANES] * (DQK ** -0.5)
    qm = jnp.stack([jnp.where(mine[h], qp(bi, h), 0.0) for bi, h in groups])
    km = jnp.stack([jnp.where(mine[h], kp(bi, h), 0.0) for bi, h in groups])
    kpb = jnp.stack([kp(bi, h) for bi, h in groups]).astype(BF16)
    v3 = jnp.stack([bv_ref[bi, :, h * DV:(h + 1) * DV] for bi, h in groups])
    rep = lambda col: jnp.broadcast_to(col, (el, LANES))
    fhc = jnp.stack([rep(fh_cols[bi][:, H_B + h:H_B + h + 1]) for bi, h in groups])
    igc = jnp.stack([rep(gc[bi][:, h:h + 1]) for bi, h in groups])
    fhr = jnp.stack([fh_rows[bi][H_B + h:H_B + h + 1, :] for bi, h in groups])
    igr = jnp.stack([gr[bi][h:h + 1, :] for bi, h in groups])
    m0 = jnp.stack([m_ref[bi, h:h + 1, :] for bi, h in groups])
    n0 = jnp.stack([n_ref[bi, h:h + 1, :] for bi, h in groups])
    c0 = c_ref[...].reshape(nb * H_B, DV, LANES)

    dmat = jnp.where(causal[None], fhc[:, :, :el] - fhr + igr, NEG_INF)
    b_in = fhc + m0
    m = jnp.maximum(b_in, jnp.max(dmat, axis=2, keepdims=True))
    wts = jnp.exp(dmat - m[:, :, :el])
    inter = jnp.exp(b_in - m)
    qmb = qm.astype(BF16)
    s = (_bdot(qmb, kpb, 2, 2) * wts).astype(BF16)
    sv = _bdot(s, jnp.concatenate([v3, jnp.ones_like(v3)], axis=2).astype(BF16), 2, 1)
    qc = _bdot(qmb, jnp.concatenate([c0, jnp.broadcast_to(n0, c0.shape)], axis=1).astype(BF16), 2, 2)
    num = sv[:, :, :DV] + inter * qc[:, :, :DV]
    den = sv[:, :, DV:] + inter * qc[:, :, DV:]
    hn = _rms(num / jnp.maximum(jnp.abs(den), jnp.exp(-m)))

    m_new = m[:, el - 1:el, :]
    fl = fhc[:, el - 1:el, :]
    wl = jnp.exp(fl - fhc + igc - m_new)
    decay = jnp.exp(fl + m0 - m_new)
    c_new = decay * c0 + _bdot((v3 * wl).astype(BF16), km.astype(BF16), 1, 1)
    n_new = decay * n0 + jnp.sum(wl * km, axis=1, keepdims=True)
    c_ref[...] = c_new.reshape(nb, H_B, DV, LANES)
    for g, (bi, h) in enumerate(groups):
        hb_ref[bi, :, h * DV:(h + 1) * DV] = hn[g]
        n_ref[bi, h:h + 1, :] = n_new[g]
        m_ref[bi, h:h + 1, :] = m_new[g]


def _mlstm(u, bv, gc, gr, conv_w, conv_b, c0, n0, m0, t0, el):
    b, s, _ = u.shape
    nc = s // el
    assert s % el == 0 and gr.shape == (b, 8, s)
    st = lambda shape: pl.BlockSpec((b,) + shape, lambda c: (0,) * (len(shape) + 1))
    seq = lambda w: pl.BlockSpec((b, el, w), lambda c: (0, c, 0))
    outs = [jax.ShapeDtypeStruct((b, s, W_BV), F32),
            jax.ShapeDtypeStruct((b, H_B, DV, LANES), F32),
            jax.ShapeDtypeStruct((b, H_B, LANES), F32),
            jax.ShapeDtypeStruct((b, H_B, LANES), F32),
            jax.ShapeDtypeStruct((b, 8, W_U), F32)]
    return pl.pallas_call(
        functools.partial(_mlstm_kernel, el=el, nb=b),
        grid=(nc,),
        in_specs=[seq(W_U), seq(W_BV), seq(LANES),
                  pl.BlockSpec((b, 8, el), lambda c: (0, 0, c)),
                  pl.BlockSpec((CONV_W, W_U), lambda c: (0, 0)),
                  pl.BlockSpec((1, W_U), lambda c: (0, 0)),
                  st((H_B, DV, LANES)), st((H_B, LANES)), st((H_B, LANES)), st((8, W_U))],
        out_specs=[seq(W_BV), st((H_B, DV, LANES)), st((H_B, LANES)), st((H_B, LANES)), st((8, W_U))],
        out_shape=outs,
        scratch_shapes=[pltpu.VMEM((b, el + 8, W_U), F32)],
        compiler_params=_params(1),
        name="mlstm_scan",
    )(u, bv, gc, gr, conv_w, conv_b, c0, n0, m0, t0)


def _mixer_kernel(x_ref, ya_ref, hb_ref, g_ref, wg_ref, bg_ref, wbo_ref, mh_ref, pa_ref, pb_ref, wo_ref,
                  fg_ref, fwin_ref, fwout_ref, o_ref, *, d_ff, ck):
    x = x_ref[...]
    d = x.shape[1]
    h = (_rms(x) * g_ref[...]).astype(BF16)
    bo = _dot(h, wbo_ref[...])
    yb = (hb_ref[...] * mh_ref[...] * _sigmoid(bo)).astype(BF16)
    pa = _dot(ya_ref[...], pa_ref[...])
    pb = _dot(yb, pb_ref[...])
    ga = _sigmoid(_dot(h, wg_ref[:, :d]) + bg_ref[:, :d])
    gb = _sigmoid(_dot(h, wg_ref[:, d:]) + bg_ref[:, d:])
    merged = (ga * pa + gb * pb).astype(BF16)
    x2 = x + _dot(merged, wo_ref[...])
    o_ref[...] = _ffn_rows(x2, fg_ref, fwin_ref, fwout_ref, d_ff, ck)


def _mixer_ffn(x, ya, hbn, mix_norm, wp, ffn_norm, ffn_w_in, ffn_w_out, tm):
    m, d = x.shape
    d_ff = ffn_w_out.shape[0]
    ck = 256
    assert m % tm == 0 and d_ff % ck == 0
    row = lambda w: pl.BlockSpec((tm, w), lambda i: (i, 0))
    return pl.pallas_call(
        functools.partial(_mixer_kernel, d_ff=d_ff, ck=ck),
        grid=(m // tm,),
        in_specs=[row(d), row(W_A), row(W_BV), _const_spec((1, d)),
                  _const_spec((d, 2 * d)), _const_spec((1, 2 * d)), _const_spec((d, W_BV)),
                  _const_spec((1, W_BV)), _const_spec((W_A, d)), _const_spec((W_BV, d)),
                  _const_spec((d, d)),
                  _const_spec((1, d)), _const_spec((d, 2 * d_ff)), _const_spec((d_ff, d))],
        out_specs=row(d),
        out_shape=jax.ShapeDtypeStruct((m, d), F32),
        compiler_params=_params(1),
        name="mixer_ffn",
    )(x, ya, hbn, mix_norm, wp["wg"], wp["bg"], wp["wbo"], wp["mh"], wp["pa"], wp["pb"], wp["wo"],
      ffn_norm, ffn_w_in, ffn_w_out)


def _decode_kernel(pt_ref, qrow_ref, q_ref, kn_ref, vn_ref, lam_ref, sn_ref, *refs, pps):
    k_refs = refs[:pps]
    v_refs = refs[pps:2 * pps]
    o_ref, m_s, l_s, acc_s = refs[2 * pps:]
    g = pl.program_id(1)
    ng = pl.num_programs(1)
    nsub = 2 * H_A
    page = k_refs[0].shape[1]

    @pl.when(g == 0)
    def _():
        m_s[...] = jnp.full(m_s.shape, NEG_INF, F32)
        l_s[...] = jnp.zeros(l_s.shape, F32)
        acc_s[...] = jnp.zeros(acc_s.shape, F32)

    row = lax.broadcasted_iota(jnp.int32, (nsub, W_A), 0)
    lane = lax.broadcasted_iota(jnp.int32, (nsub, W_A), 1)
    qbd = jnp.where(lane // D_HEAD == row, jnp.broadcast_to(qrow_ref[0].astype(F32), (nsub, W_A)), 0.0).astype(BF16)

    s = jnp.concatenate([_dot(qbd, k_refs[i][...].astype(BF16)) for i in range(pps)], axis=1)
    m = m_s[...]
    m_new = jnp.maximum(m, jnp.max(s, axis=1, keepdims=True))
    alpha = jnp.exp2(m - m_new)
    p = jnp.exp2(s - m_new)
    l_s[...] = alpha * l_s[...] + jnp.sum(p, axis=1, keepdims=True)
    m_s[...] = m_new
    pb = p.astype(BF16)
    for h in range(H_A):
        pv = _dot(pb[:, 0:page], v_refs[0][pl.ds(h, page, stride=H_A), :].astype(BF16))
        for i in range(1, pps):
            pv = pv + _dot(pb[:, i * page:(i + 1) * page], v_refs[i][pl.ds(h, page, stride=H_A), :].astype(BF16))
        acc_s[h] = alpha * acc_s[h] + pv

    @pl.when(g == ng - 1)
    def _():
        q8 = q_ref[0].astype(F32)
        s_self = jnp.sum(q8 * kn_ref[0].astype(BF16).astype(F32), axis=1, keepdims=True)
        m = m_s[...]
        m_new = jnp.maximum(m, s_self)
        alpha = jnp.exp2(m - m_new)
        p_self = jnp.exp2(s_self - m_new)
        l = alpha * l_s[...] + p_self
        lam = _lambda(lam_ref)
        vn = vn_ref[0].astype(BF16).astype(F32)
        outs = []
        for h in range(H_A):
            o = (alpha * acc_s[h] + p_self * vn[h:h + 1, :]) / l
            oa = o[2 * h:2 * h + 1, :] - lam * o[2 * h + 1:2 * h + 2, :]
            outs.append(_rms(oa) * sn_ref[...] * (1.0 - LAM_INIT))
        o_ref[0] = jnp.concatenate(outs, axis=1).astype(BF16)


def _decode_attn(page_table, q, k_new, v_new, cache_k, cache_v, lam, sub_norm, pps):
    b, n_pages = page_table.shape
    n_pool, page = cache_k.shape[1], cache_k.shape[2]
    nsub = 2 * H_A
    assert n_pages % pps == 0 and cache_k.shape[0] == 1
    pt = page_table.reshape(-1)
    ck = jnp.transpose(cache_k[0], (0, 2, 3, 1)).reshape(n_pool, nsub * D_HEAD, page)
    cv = cache_v[0].reshape(n_pool, page * H_A, 2 * D_HEAD)

    kspec = lambda i: pl.BlockSpec((None, nsub * D_HEAD, page),
                                   lambda b, g, pt: (pt[b * n_pages + g * pps + i], 0, 0))
    vspec = lambda i: pl.BlockSpec((None, page * H_A, 2 * D_HEAD),
                                   lambda b, g, pt: (pt[b * n_pages + g * pps + i], 0, 0))
    tok = lambda heads, width: pl.BlockSpec((1, heads, width), lambda b, g, pt: (b, 0, 0))
    grid_spec = pltpu.PrefetchScalarGridSpec(
        num_scalar_prefetch=1,
        grid=(b, n_pages // pps),
        in_specs=[tok(1, W_A), tok(nsub, D_HEAD), tok(nsub, D_HEAD), tok(H_A, 2 * D_HEAD),
                  pl.BlockSpec((4, D_HEAD), lambda b, g, pt: (0, 0)),
                  pl.BlockSpec((1, LANES), lambda b, g, pt: (0, 0))]
                 + [kspec(i) for i in range(pps)] + [vspec(i) for i in range(pps)],
        out_specs=pl.BlockSpec((1, 1, W_A), lambda b, g, pt: (b, 0, 0)),
        scratch_shapes=[pltpu.VMEM((nsub, 1), F32), pltpu.VMEM((nsub, 1), F32),
                        pltpu.VMEM((H_A, nsub, 2 * D_HEAD), F32)],
    )
    return pl.pallas_call(
        functools.partial(_decode_kernel, pps=pps),
        grid_spec=grid_spec,
        out_shape=jax.ShapeDtypeStruct((b, 1, W_A), BF16),
        compiler_params=_params(2),
        name="decode_attn",
    )(pt, q.reshape(b, 1, W_A), q, k_new, v_new, lam, sub_norm, *([ck] * pps), *([cv] * pps))


def _mlstm_step_kernel(u_ref, cs_ref, bv_ref, gc_ref, cw_ref, cb_ref, c0_ref, n0_ref, m0_ref,
                       hb_ref, c_ref, n_ref, m_ref, cso_ref, *, nb):
    u = u_ref[...]
    cs = cs_ref[...]
    conv = cb_ref[...] + u * cw_ref[CONV_W - 1:CONV_W, :]
    for j in range(CONV_W - 1):
        conv = conv + cs[:, j * W_U:(j + 1) * W_U] * cw_ref[j:j + 1, :]
    cso_ref[...] = jnp.concatenate([cs[:, W_U:], u], axis=1)
    qk = conv * _sigmoid(conv)
    gc = gc_ref[...]
    bv = bv_ref[...]
    m0_all = m0_ref[...]
    ident = jnp.where(lax.broadcasted_iota(jnp.int32, (DQK, DQK), 0) == lax.broadcasted_iota(jnp.int32, (DQK, DQK), 1),
                      1.0, 0.0).astype(BF16)
    m_cols = []
    for h in range(H_B):
        qh = qk[:, h * DQK:(h + 1) * DQK]
        kh = qk[:, H_B * DQK + h * DQK:H_B * DQK + (h + 1) * DQK] * (DQK ** -0.5)
        vh = bv[:, h * DV:(h + 1) * DV]
        ig = gc[:, h:h + 1]
        logf = gc[:, H_B + h:H_B + h + 1]
        m0 = m0_all[:, h:h + 1]
        n0 = n0_ref[:, h, :]
        b_in = logf + m0
        m = jnp.maximum(b_in, ig)
        wts = jnp.exp(ig - m)
        inter = jnp.exp(b_in - m)
        s = jnp.sum(qh * kh, axis=1, keepdims=True) * wts
        qhb = qh.astype(BF16)
        cq_rows = []
        for b in range(nb):
            cq_rows.append(_dot(qhb, c0_ref[b, h].astype(BF16))[b:b + 1, :])
        cq = jnp.concatenate(cq_rows, axis=0)
        num = s * vh + inter * cq
        den = s + inter * jnp.sum(n0 * qh, axis=1, keepdims=True)
        hout = num / jnp.maximum(jnp.abs(den), jnp.exp(-m))
        hb_ref[:, h * DV:(h + 1) * DV] = _rms(hout)
        kcols = _nt(ident, kh.astype(BF16))
        vw = vh * wts
        for b in range(nb):
            c_ref[b, h] = inter[b:b + 1, :] * c0_ref[b, h] + kcols[:, b:b + 1] * vw[b:b + 1, :]
        n_ref[:, h, :] = inter * n0 + wts * kh
        m_cols.append(m)
    m_ref[...] = jnp.concatenate(m_cols, axis=1)


def _mlstm_step(u, conv_state, bv, gc, conv_w, conv_b, c0, n0, m0, nb):
    b = u.shape[0]
    assert b % nb == 0
    row = lambda w: pl.BlockSpec((nb, w), lambda i: (i, 0))
    outs = [jax.ShapeDtypeStruct((b, W_BV), F32),
            jax.ShapeDtypeStruct((b, H_B, DQK, DV), F32),
            jax.ShapeDtypeStruct((b, H_B, DQK), F32),
            jax.ShapeDtypeStruct((b, H_B), F32),
            jax.ShapeDtypeStruct((b, (CONV_W - 1) * W_U), F32)]
    cspec = pl.BlockSpec((nb, H_B, DQK, DV), lambda i: (i, 0, 0, 0))
    nspec = pl.BlockSpec((nb, H_B, DQK), lambda i: (i, 0, 0))
    return pl.pallas_call(
        functools.partial(_mlstm_step_kernel, nb=nb),
        grid=(b // nb,),
        in_specs=[row(W_U), row((CONV_W - 1) * W_U), row(W_BV), row(LANES),
                  pl.BlockSpec((CONV_W, W_U), lambda i: (0, 0)),
                  pl.BlockSpec((1, W_U), lambda i: (0, 0)),
                  cspec, nspec, row(H_B)],
        out_specs=[row(W_BV), cspec, nspec, row(H_B), row((CONV_W - 1) * W_U)],
        out_shape=outs,
        compiler_params=_params(1),
        name="mlstm_step",
    )(u, conv_state, bv, gc, conv_w, conv_b, c0, n0, m0)


def _rope_tables(pos):
    half = D_HEAD // 2
    inv = 1.0 / (ROPE_THETA ** (jnp.arange(half, dtype=F32) / half))
    ang = pos.astype(F32)[:, None] * inv[None, :]
    cos = jnp.cos(ang)
    sin = jnp.sin(ang)
    return jnp.tile(cos, (1, 4)), jnp.tile(jnp.concatenate([-sin, sin], axis=1), (1, 2))


def kernel(x_prompt, x_sample, cache_k, cache_v, page_table, state_C, state_n, state_m, state_conv, meta_tokens, ffn1_norm, ffn1_w_in, ffn1_w_out, mix_norm, w_in, b_if, b_gate, q_norm, k_norm, lam_q1, lam_k1, lam_q2, lam_k2, sub_norm, conv_w, conv_b, mh_norm, p_a, p_b, w_o, ffn2_norm, ffn2_w_in, ffn2_w_out):
    bp, s, d = x_prompt.shape
    bs = x_sample.shape[0]
    assert x_sample.shape[1] == 1 and w_in.shape[0] == 1
    n_pages = page_table.shape[1]
    page = cache_k.shape[2]

    w = w_in[0]
    o = 0
    cols = {}
    for name, width in (("q", W_A), ("k", W_A), ("v", W_A), ("u", W_U), ("bv", W_BV), ("bo", W_BV),
                        ("if", 2 * H_B), ("g", 2 * d)):
        cols[name] = w[:, o:o + width]
        o += width
    gidx = jnp.arange(2 * LANES) // D_HEAD
    wp = dict(
        wqk=jnp.concatenate([cols["q"], cols["k"]], axis=1).astype(BF16),
        wv=cols["v"].astype(BF16), wu=cols["u"].astype(BF16), wbv=cols["bv"].astype(BF16),
        wif=jnp.pad(cols["if"], ((0, 0), (0, LANES - 2 * H_B))).astype(BF16),
        bif=jnp.pad(b_if[0], (0, LANES - 2 * H_B)).reshape(1, LANES),
        qn=jnp.tile(q_norm[0], LANES // D_HEAD).reshape(1, LANES),
        kn=jnp.tile(k_norm[0], LANES // D_HEAD).reshape(1, LANES),
        gmat=jnp.where(gidx[:, None] == gidx[None, :], 1.0 / D_HEAD, 0.0).astype(BF16),
        wg=cols["g"].astype(BF16), bg=b_gate[0].reshape(1, 2 * d), wbo=cols["bo"].astype(BF16),
        mh=mh_norm[0].reshape(1, W_BV), pa=p_a[0].astype(BF16), pb=p_b[0].astype(BF16),
        wo=w_o[0].astype(BF16),
    )
    f1 = (ffn1_norm[0].reshape(1, d), ffn1_w_in[0].astype(BF16), ffn1_w_out[0].astype(BF16))
    f2 = (ffn2_norm[0].reshape(1, d), ffn2_w_in[0].astype(BF16), ffn2_w_out[0].astype(BF16))
    mixn = mix_norm[0].reshape(1, d)
    lam = jnp.stack([lam_q1[0], lam_k1[0], lam_q2[0], lam_k2[0]])
    subn = sub_norm[0].reshape(1, 2 * D_HEAD)
    cw, cb = conv_w[0], conv_b[0].reshape(1, W_U)

    tm = 512 if (bp * s) % 512 == 0 else 256
    xr = x_prompt.reshape(bp * s, d)
    n_small = -(-(bs + N_META) // 256) * 256
    xs = jnp.concatenate([x_sample.reshape(bs, d), meta_tokens.astype(F32),
                          jnp.zeros((n_small - bs - N_META, d), F32)], axis=0)
    past = n_pages * page
    pos_small = jnp.concatenate([jnp.full((bs,), past, jnp.int32), jnp.arange(N_META, dtype=jnp.int32),
                                 jnp.zeros((n_small - bs - N_META,), jnp.int32)])
    cos_r, sin_r = _rope_tables(N_META + jnp.arange(s))
    cos_s, sin_s = _rope_tables(pos_small)

    x1s = _ffn(xs, *f1, n_small)
    q_s, kt_s, _, v4_s, _, u_s, bv_s, gc_s, gr_s = _proj(x1s, mixn, wp, cos_s, sin_s, n_small, 1)
    v4_s = v4_s[0]
    ssl = slice(0, bs)
    msl = slice(bs, bs + N_META)
    k_new = kt_s[0, :, ssl].T.reshape(bs, 2 * H_A, D_HEAD)
    v_new = v4_s[:bs * H_A].reshape(bs, H_A, 2 * D_HEAD)

    x1r = _ffn(xr, *f1, tm)
    q_r, kt_r, kb_r, v4_r, vb_r, u_r, bv_r, gc_r, gr_r = _proj(x1r, mixn, wp, cos_r, sin_r, tm, s // tm, N_META)

    kt_meta = kt_s[0, :, msl]
    v4_meta = v4_s[bs * H_A:(bs + N_META) * H_A]
    k_meta, v_meta = kt_meta.T, v4_meta.reshape(N_META, W_A)
    pad_meta = lambda a: jnp.pad(a, ((0, LANES - N_META), (0, 0))).astype(BF16)
    _, c_m, n_m, m_m, t_m = _mlstm(
        u_s[msl][None], bv_s[msl][None], gc_s[msl][None], gr_s[:, msl][None], cw, cb,
        jnp.zeros((1, H_B, DV, LANES), F32), jnp.zeros((1, H_B, LANES), F32),
        jnp.full((1, H_B, LANES), NEG_INF, F32), jnp.zeros((1, 8, W_U), F32), N_META)

    tq = 1024 if s % 1024 == 0 else (512 if s % 512 == 0 else 256)
    ya_r =_prompt_attn(q_r.reshape(bp, s, W_A), kb_r.reshape(bp, s, W_A), vb_r.reshape(bp, s, W_A),
                        pad_meta(k_meta), pad_meta(v_meta), lam, subn, tq, 1)
    rep = lambda a: jnp.broadcast_to(a, (bp,) + a.shape[1:])
    hb_r, c_p, n_p, m_p, _ = _mlstm(
        u_r.reshape(bp, s, W_U), bv_r.reshape(bp, s, W_BV), gc_r.reshape(bp, s, LANES),
        jnp.swapaxes(gr_r.reshape(8, bp, s), 0, 1), cw, cb,
        rep(c_m), rep(n_m), rep(m_m), rep(t_m), CHUNK)

    pps = 32 if n_pages % 32 == 0 else n_pages
    ya_s = _decode_attn(page_table, q_s[ssl].reshape(bs, 2 * H_A, D_HEAD), k_new, v_new,
                        cache_k, cache_v, lam, subn, pps)
    hb_s, c_s, n_s, m_s, conv_s = _mlstm_step(
        u_s[ssl], state_conv[0].reshape(bs, (CONV_W - 1) * W_U), bv_s[ssl], gc_s[ssl], cw, cb,
        jnp.swapaxes(state_C[0], 2, 3), state_n[0], state_m[0], 8)

    y_r = _mixer_ffn(x1r, ya_r.reshape(bp * s, W_A), hb_r.reshape(bp * s, W_BV), mixn, wp, *f2, tm)
    y_s = _mixer_ffn(x1s[ssl], ya_s.reshape(bs, W_A), hb_s, mixn, wp, *f2, bs)

    def unpad_heads(a):
        return jnp.stack([a[:, h, ..., (h % 2) * DQK:(h % 2 + 1) * DQK] for h in range(H_B)], axis=1)

    kt_all = jnp.concatenate([jnp.broadcast_to(kt_meta[None], (bp, W_A, N_META)), kt_r], axis=2)
    k_prompt = jnp.transpose(kt_all.reshape(bp, 2 * H_A, D_HEAD, N_META + s), (0, 3, 1, 2))[None]
    v4_all = v4_r.at[:, :N_META * H_A].set(jnp.broadcast_to(v4_meta[None], (bp, N_META * H_A, 2 * D_HEAD)))
    v_prompt = v4_all.reshape(1, bp, N_META + s, H_A, 2 * D_HEAD)
    return (y_r.reshape(bp, s, d), y_s.reshape(bs, 1, d), k_prompt, v_prompt,
            k_new.reshape(1, bs, 1, 2 * H_A, D_HEAD), v_new.reshape(1, bs, 1, H_A, 2 * D_HEAD),
            unpad_heads(c_p)[None], unpad_heads(n_p[:, :, None, :])[:, :, 0][None], m_p[:, :, 0][None],
            u_r.reshape(bp, s, W_U)[:, s - (CONV_W - 1):][None],
            jnp.swapaxes(c_s, 2, 3)[None], n_s[None], m_s[None], conv_s.reshape(1, bs, CONV_W - 1, W_U))
```
